```python
import math
import jax
import jax.numpy as jnp
from jax import lax
import numpy as np


D_MODEL = 2048
BATCH = 8
SEQ = 4096
DEPTH = 2

GRID_W = 64
CTX_LEN = 256
D_FF = 5632
N_MOD = 9
N_NORM = 6
MIX_A = D_MODEL // 2
MIX_B = D_MODEL // 2
CONV_W = 3
POOL_WINDOWS = (2, 4, 8, 16)
POOL_GROUPS = len(POOL_WINDOWS)
POOL_CH = MIX_B // POOL_GROUPS
IN_COLS = 3 * MIX_A + MIX_B
S5_GROUP_CH = 16
S5_GROUPS = D_MODEL // S5_GROUP_CH
S5_STATE = 64
DT_MIN = 1e-3
DT_MAX = 1e-1
N_EVEN = (DEPTH + 1) // 2
N_ODD = DEPTH // 2
EPS = 1e-6

kernel_name = 'hybrid_conv_pool_s5_flow_block'


def rmsnorm(x, g):
    xf = x.astype(jnp.float32)
    y = xf * lax.rsqrt(jnp.mean(xf * xf, axis=-1, keepdims=True) + EPS) * g.astype(jnp.float32)
    return y.astype(x.dtype)


def pre(x, m, g, i):
    return rmsnorm(x, g[2 * i]) * (1 + m[3 * i + 1]) + m[3 * i]


def post(x, y, m, g, i, weight):
    return x + weight * m[3 * i + 2] * rmsnorm(y, g[2 * i + 1])


def swiglu(h, wg, wu, wd):
    return (jax.nn.silu(h @ wg) * (h @ wu)) @ wd


def half_ffn(x, m, g, i, wg, wu, wd):
    return post(x, swiglu(pre(x, m, g, i), wg, wu, wd), m, g, i, 0.5)


def shift_conv3(z, w):
    zp = jnp.pad(z, ((0, 0), (1, 1), (0, 0)))
    return w[0] * zp[:, :-2] + w[1] * z + w[2] * zp[:, 2:]


def pool_minus_self(u, w, axis):
    n = u.shape[axis]
    uf = u.astype(jnp.float32)
    pad = [(0, 0)] * u.ndim
    pad[axis] = (1, 0)
    csum = jnp.pad(jnp.cumsum(uf, axis=axis), pad)
    j = np.arange(n)
    lo = np.clip(j - w // 2, 0, n)
    hi = np.clip(j + w // 2, 0, n)
    total = jnp.take(csum, hi, axis=axis) - jnp.take(csum, lo, axis=axis)
    cnt_shape = [1] * u.ndim
    cnt_shape[axis] = n
    cnt = jnp.asarray((hi - lo).astype(np.float32).reshape(cnt_shape))
    return (total / cnt - uf).astype(u.dtype)


def multiscale_pool(u, grid):
    b, n, _ = u.shape
    ug = u.reshape(b, n, POOL_GROUPS, POOL_CH)
    outs = []
    for gi, w in enumerate(POOL_WINDOWS):
        ui = ug[:, :, gi]
        if grid:
            rows = n // GRID_W
            pi = pool_minus_self(ui.reshape(b, rows, GRID_W, POOL_CH), w, axis=2).reshape(b, n, POOL_CH)
        else:
            pi = pool_minus_self(ui, w, axis=1)
        outs.append(pi)
    return jnp.stack(outs, axis=2)


def even_mixer(h, w_in, conv_w, pool_w, pool_scale, w_out, grid):
    p = h @ w_in
    gate_b, gate_c, v, u = jnp.split(p, [MIX_A, 2 * MIX_A, 3 * MIX_A], axis=-1)
    y_a = gate_b * shift_conv3(gate_c * v, conv_w)
    pooled = multiscale_pool(u, grid)
    y_b = jnp.einsum('blgc,gcd->blgd', pooled, pool_w).reshape(u.shape) * pool_scale
    return jnp.concatenate([y_a, y_b], axis=-1) @ w_out


def complex_mul(ar, ai, br, bi):
    return ar * br - ai * bi, ar * bi + ai * br


def s5_discretise(lam_re, lam_im, log_step, b_re, b_im):
    lam_re = lam_re.astype(jnp.float32)
    lam_im = lam_im.astype(jnp.float32)
    dt = jnp.exp(log_step.astype(jnp.float32))[..., None]
    mag = jnp.exp(lam_re * dt)
    lbar_re = mag * jnp.cos(lam_im * dt)
    lbar_im = mag * jnp.sin(lam_im * dt)
    den = lam_re * lam_re + lam_im * lam_im
    nr = lbar_re - 1.0
    fr = (nr * lam_re + lbar_im * lam_im) / den
    fi = (lbar_im * lam_re - nr * lam_im) / den
    bbar_re, bbar_im = complex_mul(fr[..., None], fi[..., None], b_re.astype(jnp.float32), b_im.astype(jnp.float32))
    return lbar_re, lbar_im, bbar_re, bbar_im


def scan_combine(e1, e2):
    a1r, a1i, b1r, b1i = e1
    a2r, a2i, b2r, b2i = e2
    ar, ai = complex_mul(a1r, a1i, a2r, a2i)
    br, bi = complex_mul(a2r, a2i, b1r, b1i)
    return ar, ai, br + b2r, bi + b2i


def diag_scan(lbar_re, lbar_im, bu_re, bu_im, reverse, s0=None):
    if s0 is not None:
        idx = -1 if reverse else 0
        inj_re, inj_im = complex_mul(lbar_re, lbar_im, s0[0], s0[1])
        bu_re = bu_re.at[idx].add(inj_re)
        bu_im = bu_im.at[idx].add(inj_im)
    a_re = jnp.broadcast_to(lbar_re, bu_re.shape)
    a_im = jnp.broadcast_to(lbar_im, bu_im.shape)
    _, _, s_re, s_im = lax.associative_scan(scan_combine, (a_re, a_im, bu_re, bu_im), reverse=reverse, axis=0)
    return s_re, s_im


def s5_mixer(h_lat, h_ctx, lam_re, lam_im, log_step, b_re, b_im, c_re, c_im, d_skip, w_a, w_b, ctx_out):
    lbr, lbi, bbr, bbi = s5_discretise(lam_re, lam_im, log_step, b_re, b_im)
    cr = c_re.astype(jnp.float32)
    ci = c_im.astype(jnp.float32)
    dv = d_skip.astype(jnp.float32)

    def input_drive(u, d):
        return jnp.einsum('lgh,gph->lgp', u, bbr[d]), jnp.einsum('lgh,gph->lgp', u, bbi[d])

    def readout(s, d):
        re = jnp.einsum('lgp,ghp->lgh', s[0], cr[d]) - jnp.einsum('lgp,ghp->lgh', s[1], ci[d])
        return re.reshape(re.shape[0], D_MODEL)

    def per_sample(args):
        u_lat, u_ctx = args
        ul = u_lat.reshape(-1, S5_GROUPS, S5_GROUP_CH)
        uc = u_ctx.reshape(-1, S5_GROUPS, S5_GROUP_CH)
        y_lat = u_lat * dv
        y_ctx = u_ctx * dv if ctx_out else None
        for d, rev in ((0, False), (1, True)):
            s_c = diag_scan(lbr[d], lbi[d], *input_drive(uc, d), rev)
            end = 0 if rev else -1
            s_l = diag_scan(lbr[d], lbi[d], *input_drive(ul, d), rev, s0=(s_c[0][end], s_c[1][end]))
            y_lat = y_lat + readout(s_l, d)
            if ctx_out:
                y_ctx = y_ctx + readout(s_c, d)
        return (y_lat, y_ctx) if ctx_out else y_lat

    ys = lax.map(per_sample, (h_lat.astype(jnp.float32), h_ctx.astype(jnp.float32)))

    def glu(y):
        gy = jax.nn.gelu(y)
        return (gy @ w_a) * jax.nn.sigmoid(gy @ w_b)

    if ctx_out:
        y_lat, y_ctx = ys
        return glu(y_lat.astype(h_lat.dtype)), glu(y_ctx.astype(h_ctx.dtype))
    return glu(ys.astype(h_lat.dtype)), None


def setup_inputs(seed: int = 0) -> dict:
    key = jax.random.key(seed)
    ks = jax.random.split(key, 25)
    f32 = jnp.float32

    def nrm(k, shape, scale):
        return scale * jax.random.normal(k, shape, f32)

    n = jnp.arange(S5_STATE, dtype=f32)
    s5_shape = (N_ODD, 2, S5_GROUPS, S5_STATE)
    b_shape = (N_ODD, 2, S5_GROUPS, S5_STATE, S5_GROUP_CH)
    c_shape = (N_ODD, 2, S5_GROUPS, S5_GROUP_CH, S5_STATE)
    return {
        'x': nrm(ks[0], (BATCH, SEQ, D_MODEL), 1.0),
        'c': nrm(ks[1], (BATCH, D_MODEL), 1.0),
        'ctx': nrm(ks[2], (BATCH, CTX_LEN, D_MODEL), 1.0),
        'c_ctx': nrm(ks[3], (D_MODEL,), 1.0),
        'w_mod': nrm(ks[4], (DEPTH, D_MODEL, N_MOD * D_MODEL), 0.2 * D_MODEL ** -0.5),
        'b_mod': nrm(ks[5], (DEPTH, N_MOD * D_MODEL), 0.01),
        'norm_g': 1.0 + nrm(ks[6], (DEPTH, N_NORM, D_MODEL), 0.02),
        'ffn_w_gate': nrm(ks[7], (DEPTH, 2, D_MODEL, D_FF), D_MODEL ** -0.5),
        'ffn_w_up': nrm(ks[8], (DEPTH, 2, D_MODEL, D_FF), D_MODEL ** -0.5),
        'ffn_w_down': nrm(ks[9], (DEPTH, 2, D_FF, D_MODEL), D_FF ** -0.5),
        'mix_in': nrm(ks[10], (N_EVEN, D_MODEL, IN_COLS), D_MODEL ** -0.5),
        'conv_w': nrm(ks[11], (N_EVEN, CONV_W, MIX_A), CONV_W ** -0.5),
        'pool_w': nrm(ks[12], (N_EVEN, POOL_GROUPS, POOL_CH, POOL_CH), POOL_CH ** -0.5),
        'pool_scale': 1.0 + nrm(ks[13], (N_EVEN, MIX_B), 0.1),
        'mix_out': nrm(ks[14], (N_EVEN, MIX_A + MIX_B, D_MODEL), (MIX_A + MIX_B) ** -0.5),
        's5_lambda_re': -0.5 + nrm(ks[15], s5_shape, 0.01),
        's5_lambda_im': math.pi * n + nrm(ks[16], s5_shape, 0.01),
        's5_log_step': jax.random.uniform(ks[17], (N_ODD, 2, S5_GROUPS), f32, math.log(DT_MIN), math.log(DT_MAX)),
        's5_b_re': nrm(ks[18], b_shape, (2 * S5_GROUP_CH) ** -0.5),
        's5_b_im': nrm(ks[19], b_shape, (2 * S5_GROUP_CH) ** -0.5),
        's5_c_re': nrm(ks[20], c_shape, S5_STATE ** -0.5),
        's5_c_im': nrm(ks[21], c_shape, S5_STATE ** -0.5),
        's5_d': nrm(ks[22], (N_ODD, D_MODEL), 1.0),
        'glu_w_a': nrm(ks[23], (N_ODD, D_MODEL, D_MODEL), D_MODEL ** -0.5),
        'glu_w_b': nrm(ks[24], (N_ODD, D_MODEL, D_MODEL), D_MODEL ** -0.5),
    }


def reference(x, c, ctx, c_ctx, w_mod, b_mod, norm_g, ffn_w_gate, ffn_w_up, ffn_w_down, mix_in, conv_w, pool_w,
              pool_scale, mix_out, s5_lambda_re, s5_lambda_im, s5_log_step, s5_b_re, s5_b_im, s5_c_re, s5_c_im,
              s5_d, glu_w_a, glu_w_b):
    b = x.shape[0]
    x_lat, x_ctx = x, ctx
    sc = jax.nn.silu(c)
    scc = jax.nn.silu(c_ctx)
    for l in range(DEPTH):
        last = l == DEPTH - 1
        even = l % 2 == 0
        run_ctx_in = not (last and even)
        ctx_out = not last
        m_lat = (sc @ w_mod[l] + b_mod[l]).reshape(b, N_MOD, D_MODEL).transpose(1, 0, 2)[:, :, None, :]
        m_ctx = (scc @ w_mod[l] + b_mod[l]).reshape(N_MOD, 1, 1, D_MODEL)
        g = norm_g[l]
        x_lat = half_ffn(x_lat, m_lat, g, 0, ffn_w_gate[l, 0], ffn_w_up[l, 0], ffn_w_down[l, 0])
        if run_ctx_in:
            x_ctx = half_ffn(x_ctx, m_ctx, g, 0, ffn_w_gate[l, 0], ffn_w_up[l, 0], ffn_w_down[l, 0])
        h_lat = pre(x_lat, m_lat, g, 1)
        h_ctx = pre(x_ctx, m_ctx, g, 1) if run_ctx_in else None
        if even:
            e = l // 2
            y_lat = even_mixer(h_lat, mix_in[e], conv_w[e], pool_w[e], pool_scale[e], mix_out[e], True)
            y_ctx = even_mixer(h_ctx, mix_in[e], conv_w[e], pool_w[e], pool_scale[e], mix_out[e], False) if ctx_out else None
        else:
            o = l // 2
            y_lat, y_ctx = s5_mixer(h_lat, h_ctx, s5_lambda_re[o], s5_lambda_im[o], s5_log_step[o], s5_b_re[o],
                                    s5_b_im[o], s5_c_re[o], s5_c_im[o], s5_d[o], glu_w_a[o], glu_w_b[o], ctx_out)
        x_lat = post(x_lat, y_lat, m_lat, g, 1, 1.0)
        x_lat = half_ffn(x_lat, m_lat, g, 2, ffn_w_gate[l, 1], ffn_w_up[l, 1], ffn_w_down[l, 1])
        if ctx_out:
            x_ctx = post(x_ctx, y_ctx, m_ctx, g, 1, 1.0)
            x_ctx = half_ffn(x_ctx, m_ctx, g, 2, ffn_w_gate[l, 1], ffn_w_up[l, 1], ffn_w_down[l, 1])
    return x_lat
```

```python
import functools

import jax
import jax.numpy as jnp
import numpy as np
from jax import lax
from jax.experimental import pallas as pl
from jax.experimental.pallas import tpu as pltpu

EPS = 1e-6
GRID_W = 64
POOL_WINDOWS = (2, 4, 8, 16)
N_MOD = 9

F32 = jnp.float32
BF16 = jnp.bfloat16

LANES = 128
SUBLANES = 8
VMEM_LIMIT_BYTES = 56 * 1024 * 1024
ROW_SHIFT, ROW_SCALE, ROW_GATE, ROW_GPRE, ROW_GPOST = 0, 1, 2, 3, 4


def _cparams(*sem):
    return pltpu.CompilerParams(dimension_semantics=sem, vmem_limit_bytes=VMEM_LIMIT_BYTES)


def _dot(a, b):
    return jnp.dot(a, b, preferred_element_type=F32)


def _pre_rows(x, prm):
    r = lax.rsqrt(jnp.mean(x * x, axis=-1, keepdims=True) + EPS)
    return (x * r * prm[ROW_GPRE:ROW_GPRE + 1]) * (1.0 + prm[ROW_SCALE:ROW_SCALE + 1]) + prm[ROW_SHIFT:ROW_SHIFT + 1]


def _post_rows(x, y, prm, weight):
    r = lax.rsqrt(jnp.mean(y * y, axis=-1, keepdims=True) + EPS)
    return x + (weight * prm[ROW_GATE:ROW_GATE + 1]) * (y * r * prm[ROW_GPOST:ROW_GPOST + 1])


def _pick_tile(total, target, quantum=LANES):
    if total <= target:
        return total
    best = quantum
    for t in range(quantum, target + 1, quantum):
        if total % t == 0:
            best = t
    assert total % best == 0, (total, target, quantum)
    return best


def _row_chunks(total, chunk, body):
    chunk = min(chunk, total)

    def step(k, carry):
        body(pl.ds(pl.multiple_of(k * chunk, chunk), chunk))
        return carry

    lax.fori_loop(0, total // chunk, step, 0)


def _mod_kernel(c_ref, w_ref, b_ref, o_ref):
    sc = jax.nn.silu(c_ref[...]).astype(BF16)
    o_ref[0] = _dot(sc, w_ref[0].astype(BF16)) + b_ref[0]


def _modulation(c_all, w_mod, b_mod):
    depth, d, nd = w_mod.shape
    rows = c_all.shape[0]
    tn = _pick_tile(nd, 1024)
    return pl.pallas_call(
        _mod_kernel,
        grid=(depth, nd // tn),
        in_specs=[
            pl.BlockSpec((rows, d), lambda l, j: (0, 0)),
            pl.BlockSpec((1, d, tn), lambda l, j: (l, 0, j)),
            pl.BlockSpec((1, 1, tn), lambda l, j: (l, 0, j)),
        ],
        out_specs=pl.BlockSpec((1, rows, tn), lambda l, j: (l, 0, j)),
        out_shape=jax.ShapeDtypeStruct((depth, rows, nd), F32),
        compiler_params=_cparams("arbitrary", "arbitrary"),
        name="modulation",
    )(c_all, w_mod, b_mod.reshape(depth, 1, nd))


def _ffn_kernel(x_ref, prm_ref, wg_ref, wu_ref, wd_ref, o_ref, h_ref, *, nj, rc):
    j = pl.program_id(1)
    tm = x_ref.shape[0]
    prm = prm_ref[0]

    @pl.when(j == 0)
    def _():
        def body(rows):
            h_ref[rows, :] = _pre_rows(x_ref[rows, :], prm).astype(BF16)
        _row_chunks(tm, rc, body)

    h = h_ref[...]
    ga = _dot(h, wg_ref[...])
    up = _dot(h, wu_ref[...])
    t = (jax.nn.silu(ga) * up).astype(BF16)
    contrib = _dot(t, wd_ref[...])

    @pl.when(j == 0)
    def _():
        o_ref[...] = contrib

    @pl.when(j > 0)
    def _():
        o_ref[...] += contrib

    @pl.when(j == nj - 1)
    def _():
        def body(rows):
            o_ref[rows, :] = _post_rows(x_ref[rows, :], o_ref[rows, :], prm, 0.5)
        _row_chunks(tm, rc, body)


def _ffn(x2, prm, rows_per_sample, wg, wu, wd, l, k, *, tm, tf):
    n, d = x2.shape
    f = wg.shape[-1]
    tm = min(tm, rows_per_sample)
    tf = _pick_tile(f, tf)
    nj = f // tf
    tiles_per_sample = max(rows_per_sample // tm, 1)
    n_samples = prm.shape[0]
    kern = functools.partial(_ffn_kernel, nj=nj, rc=128)
    return pl.pallas_call(
        kern,
        grid=(n // tm, nj),
        in_specs=[
            pl.BlockSpec((tm, d), lambda i, j: (i, 0), pipeline_mode=pl.Buffered(1)),
            pl.BlockSpec((1, 8, d), lambda i, j: (jnp.minimum(i // tiles_per_sample, n_samples - 1), 0, 0)),
            pl.BlockSpec((None, None, d, tf), lambda i, j: (l, k, 0, j)),
            pl.BlockSpec((None, None, d, tf), lambda i, j: (l, k, 0, j)),
            pl.BlockSpec((None, None, tf, d), lambda i, j: (l, k, j, 0)),
        ],
        out_specs=pl.BlockSpec((tm, d), lambda i, j: (i, 0)),
        out_shape=jax.ShapeDtypeStruct((n, d), F32),
        scratch_shapes=[pltpu.VMEM((tm, d), BF16)],
        compiler_params=_cparams("arbitrary", "arbitrary"),
        name="ffn",
    )(x2, prm, wg, wu, wd)


def _preproj_kernel(x_ref, prm_ref, w_ref, o_ref, h_ref, *, rc):
    j = pl.program_id(1)
    tm = x_ref.shape[0]
    prm = prm_ref[0]

    @pl.when(j == 0)
    def _():
        def body(rows):
            h_ref[rows, :] = _pre_rows(x_ref[rows, :], prm).astype(BF16)
        _row_chunks(tm, rc, body)

    o_ref[...] = _dot(h_ref[...], w_ref[...])


def _preproj(x2, prm, rows_per_sample, w, e, *, tm, tn):
    n, d = x2.shape
    nc = w.shape[-1]
    tm = min(tm, rows_per_sample)
    tn = _pick_tile(nc, tn)
    tiles_per_sample = max(rows_per_sample // tm, 1)
    n_samples = prm.shape[0]
    return pl.pallas_call(
        functools.partial(_preproj_kernel, rc=128),
        grid=(n // tm, nc // tn),
        in_specs=[
            pl.BlockSpec((tm, d), lambda i, j: (i, 0)),
            pl.BlockSpec((1, 8, d), lambda i, j: (jnp.minimum(i // tiles_per_sample, n_samples - 1), 0, 0)),
            pl.BlockSpec((None, d, tn), lambda i, j: (e, 0, j)),
        ],
        out_specs=pl.BlockSpec((tm, tn), lambda i, j: (i, j)),
        out_shape=jax.ShapeDtypeStruct((n, nc), F32),
        scratch_shapes=[pltpu.VMEM((tm, d), BF16)],
        compiler_params=_cparams("arbitrary", "arbitrary"),
        name="mix_in_proj",
    )(x2, prm, w)


def _evenmix_kernel(p_ref, pprev_ref, pnext_ref, x_ref, prm_ref, cw_ref, pw_ref, ps_ref, wo_ref, o_ref,
                    cv_ref, pa_ref, pb_ref, ycat_ref, *, blk, tiles_per_sample, mix_a, pool_ch):
    i = pl.program_id(0)
    tm = x_ref.shape[0]
    prm = prm_ref[0]
    nblk = tm // blk
    pitch = blk + 2 * SUBLANES
    npad = nblk * pitch + SUBLANES
    t_in_sample = i % tiles_per_sample
    has_prev = (t_in_sample != 0).astype(F32)
    has_next = (t_in_sample != tiles_per_sample - 1).astype(F32)

    c0, c1, c2, c3 = 0, mix_a, 2 * mix_a, 3 * mix_a
    cv = p_ref[:, c1:c2] * p_ref[:, c2:c3]
    cv_ref[SUBLANES:SUBLANES + tm, :] = cv
    cv_ref[0:SUBLANES, :] = pprev_ref[:, c1:c2] * pprev_ref[:, c2:c3] * has_prev
    cv_ref[SUBLANES + tm:2 * SUBLANES + tm, :] = pnext_ref[:, c1:c2] * pnext_ref[:, c2:c3] * has_next
    conv = (cw_ref[0:1, :] * cv_ref[SUBLANES - 1:SUBLANES - 1 + tm, :] + cw_ref[1:2, :] * cv
            + cw_ref[2:3, :] * cv_ref[SUBLANES + 1:SUBLANES + 1 + tm, :])
    ycat_ref[:, 0:mix_a] = (p_ref[:, c0:c1] * conv).astype(BF16)

    pos = lax.broadcasted_iota(jnp.int32, (tm, 1), 0) % blk
    inner = npad - 2 * SUBLANES
    for gi, w in enumerate(POOL_WINDOWS):
        lo, hi = c3 + gi * pool_ch, c3 + (gi + 1) * pool_ch
        pa_ref[...] = jnp.zeros_like(pa_ref)
        for b in range(nblk):
            pa_ref[b * pitch + SUBLANES:b * pitch + SUBLANES + blk, :] = p_ref[b * blk:(b + 1) * blk, lo:hi]
        src, dst = pa_ref, pb_ref
        dst[...] = jnp.zeros_like(dst)
        dst[SUBLANES:SUBLANES + inner, :] = src[SUBLANES - 1:SUBLANES - 1 + inner, :] + src[SUBLANES:SUBLANES + inner, :]
        src, dst = dst, src
        half = 1
        while 4 * half <= w:
            dst[...] = jnp.zeros_like(dst)
            dst[SUBLANES:SUBLANES + inner, :] = (src[SUBLANES - half:SUBLANES - half + inner, :]
                                                 + src[SUBLANES + half:SUBLANES + half + inner, :])
            src, dst = dst, src
            half *= 2
        cnt = (jnp.minimum(pos + w // 2, blk) - jnp.maximum(pos - w // 2, 0)).astype(F32)
        pieces = [src[b * pitch + SUBLANES:b * pitch + SUBLANES + blk, :] for b in range(nblk)]
        total = pieces[0] if nblk == 1 else jnp.concatenate(pieces, axis=0)
        pooled = total / cnt - p_ref[:, lo:hi]
        yb = _dot(pooled.astype(BF16), pw_ref[gi]) * ps_ref[:, gi * pool_ch:(gi + 1) * pool_ch]
        ycat_ref[:, mix_a + gi * pool_ch:mix_a + (gi + 1) * pool_ch] = yb.astype(BF16)

    y = _dot(ycat_ref[...], wo_ref[...])
    o_ref[...] = _post_rows(x_ref[...], y, prm, 1.0)


def _evenmix(p2, x2, prm, rows_per_sample, conv_w8, pool_w, pool_scale, w_out, e, *, tm, blk):
    n, d = x2.shape
    nc = p2.shape[1]
    mix_a = conv_w8.shape[-1]
    pool_ch = pool_w.shape[-1]
    mix_b = pool_ch * len(POOL_WINDOWS)
    tm = min(tm, rows_per_sample)
    tiles_per_sample = rows_per_sample // tm
    n_samples = prm.shape[0]
    nblk = tm // blk
    npad = nblk * (blk + 2 * SUBLANES) + SUBLANES
    tb = tm // SUBLANES
    nb8 = n // SUBLANES
    kern = functools.partial(_evenmix_kernel, blk=blk, tiles_per_sample=tiles_per_sample, mix_a=mix_a, pool_ch=pool_ch)
    return pl.pallas_call(
        kern,
        grid=(n // tm,),
        in_specs=[
            pl.BlockSpec((tm, nc), lambda i: (i, 0)),
            pl.BlockSpec((SUBLANES, nc), lambda i: (jnp.maximum(i * tb - 1, 0), 0)),
            pl.BlockSpec((SUBLANES, nc), lambda i: (jnp.minimum((i + 1) * tb, nb8 - 1), 0)),
            pl.BlockSpec((tm, d), lambda i: (i, 0)),
            pl.BlockSpec((1, 8, d), lambda i: (jnp.minimum(i // tiles_per_sample, n_samples - 1), 0, 0)),
            pl.BlockSpec((None, 8, mix_a), lambda i: (e, 0, 0)),
            pl.BlockSpec((None, len(POOL_WINDOWS), pool_ch, pool_ch), lambda i: (e, 0, 0, 0)),
            pl.BlockSpec((None, 1, mix_b), lambda i: (e, 0, 0)),
            pl.BlockSpec((None, mix_a + mix_b, d), lambda i: (e, 0, 0)),
        ],
        out_specs=pl.BlockSpec((tm, d), lambda i: (i, 0)),
        out_shape=jax.ShapeDtypeStruct((n, d), F32),
        scratch_shapes=[
            pltpu.VMEM((tm + 2 * SUBLANES, mix_a), F32),
            pltpu.VMEM((npad, pool_ch), F32),
            pltpu.VMEM((npad, pool_ch), F32),
            pltpu.VMEM((tm, mix_a + mix_b), BF16),
        ],
        compiler_params=_cparams("arbitrary"),
        name="even_mixer",
    )(p2, p2, p2, x2, prm, conv_w8, pool_w, pool_scale, w_out)


def _prenorm_kernel(x_ref, prm_ref, o_ref, *, rc):
    prm = prm_ref[0]

    def body(rows):
        o_ref[rows, :] = _pre_rows(x_ref[rows, :], prm)
    _row_chunks(x_ref.shape[0], rc, body)


def _prenorm(x2, prm, rows_per_sample, *, tm):
    n, d = x2.shape
    tm = min(tm, rows_per_sample)
    tiles_per_sample = rows_per_sample // tm
    n_samples = prm.shape[0]
    return pl.pallas_call(
        functools.partial(_prenorm_kernel, rc=128),
        grid=(n // tm,),
        in_specs=[
            pl.BlockSpec((tm, d), lambda i: (i, 0)),
            pl.BlockSpec((1, 8, d), lambda i: (jnp.minimum(i // tiles_per_sample, n_samples - 1), 0, 0)),
        ],
        out_specs=pl.BlockSpec((tm, d), lambda i: (i, 0)),
        out_shape=jax.ShapeDtypeStruct((n, d), F32),
        compiler_params=_cparams("arbitrary"),
        name="s5_prenorm",
    )(x2, prm)


def _s5_kernel(hc_ref, hl_ref, are_ref, aim_ref, wb_ref, wc_ref, o_ref,
               slab_ref, htm_ref, bu_ref, st_ref, ytm_ref, *, reverse, n_ctx_chunks, tc, nsub, ch_sub, st_sub):
    c = pl.program_id(1)
    nb = hl_ref.shape[0]
    half = st_sub // 2
    pitch = tc + SUBLANES
    nslab = slab_ref.shape[0]

    @pl.when(c == 0)
    def _():
        st_ref[...] = jnp.zeros_like(st_ref)

    def to_time_major(src_ref):
        for b in range(nb):
            for k in range(nslab):
                slab_ref[k, b * pitch:b * pitch + tc, :] = src_ref[b, :, k * LANES:(k + 1) * LANES]

        def gather(t, carry):
            row = pl.multiple_of(t * nb, nb)
            for k in range(nslab):
                htm_ref[pl.ds(row, nb), k * LANES:(k + 1) * LANES] = slab_ref.at[k][pl.ds(t, nb, stride=pitch), :]
            return carry
        lax.fori_loop(0, tc, gather, 0)

    @pl.when(c < n_ctx_chunks)
    def _():
        to_time_major(hc_ref)

    @pl.when(c >= n_ctx_chunks)
    def _():
        to_time_major(hl_ref)

    for j in range(nsub):
        hj = htm_ref[:, j * ch_sub:(j + 1) * ch_sub].astype(BF16)
        bu_ref[:, j * st_sub:(j + 1) * st_sub] = _dot(hj, wb_ref[0, j])

    def scan_step(s, st):
        t = (tc - 1 - s) if reverse else s
        rows = pl.ds(pl.multiple_of(t * nb, nb), nb)
        new = []
        for j in range(nsub):
            lo = j * st_sub
            sr, si = st[:, lo:lo + half], st[:, lo + half:lo + st_sub]
            ar = are_ref[0, :, j * half:(j + 1) * half]
            ai = aim_ref[0, :, j * half:(j + 1) * half]
            nr = ar * sr - ai * si + bu_ref[rows, lo:lo + half]
            ni = ar * si + ai * sr + bu_ref[rows, lo + half:lo + st_sub]
            new += [nr, ni]
        st = jnp.concatenate(new, axis=1)
        bu_ref[rows, :] = st
        return st

    st_ref[...] = lax.fori_loop(0, tc, scan_step, st_ref[...])

    @pl.when(c >= n_ctx_chunks)
    def _():
        for j in range(nsub):
            sj = bu_ref[:, j * st_sub:(j + 1) * st_sub].astype(BF16)
            ytm_ref[:, j * ch_sub:(j + 1) * ch_sub] = _dot(sj, wc_ref[0, j])
        for k in range(nslab):
            slab_ref[k, 0:tc * nb, :] = ytm_ref[:, k * LANES:(k + 1) * LANES]
        for b in range(nb):
            for k in range(nslab):
                o_ref[b, :, k * LANES:(k + 1) * LANES] = slab_ref.at[k][pl.ds(b, tc, stride=nb), :]


def _s5_scan(h_ctx3, h_lat3, a_re, a_im, wb, wc, *, reverse, tc):
    nb, lc, d = h_ctx3.shape
    l = h_lat3.shape[1]
    nq, nsub, ch_sub, st_sub = wb.shape
    dq = nsub * ch_sub
    tc = min(tc, lc)
    ncc, nlc = lc // tc, l // tc
    pitch = tc + SUBLANES

    def ctx_idx(c):
        cc = jnp.minimum(c, ncc - 1)
        return (ncc - 1 - cc) if reverse else cc

    def lat_idx(c):
        lc_ = jnp.maximum(c - ncc, 0)
        return (nlc - 1 - lc_) if reverse else lc_

    kern = functools.partial(_s5_kernel, reverse=reverse, n_ctx_chunks=ncc, tc=tc, nsub=nsub, ch_sub=ch_sub, st_sub=st_sub)
    return pl.pallas_call(
        kern,
        grid=(nq, ncc + nlc),
        in_specs=[
            pl.BlockSpec((nb, tc, dq), lambda q, c: (0, ctx_idx(c), q)),
            pl.BlockSpec((nb, tc, dq), lambda q, c: (0, lat_idx(c), q)),
            pl.BlockSpec((1, nb, nsub * st_sub // 2), lambda q, c: (q, 0, 0)),
            pl.BlockSpec((1, nb, nsub * st_sub // 2), lambda q, c: (q, 0, 0)),
            pl.BlockSpec((1, nsub, ch_sub, st_sub), lambda q, c: (q, 0, 0, 0)),
            pl.BlockSpec((1, nsub, st_sub, ch_sub), lambda q, c: (q, 0, 0, 0)),
        ],
        out_specs=pl.BlockSpec((nb, tc, dq), lambda q, c: (0, lat_idx(c), q)),
        out_shape=jax.ShapeDtypeStruct((nb, l, d), F32),
        scratch_shapes=[
            pltpu.VMEM((dq // LANES, nb * pitch, LANES), F32),
            pltpu.VMEM((tc * nb, dq), F32),
            pltpu.VMEM((tc * nb, nsub * st_sub), F32),
            pltpu.VMEM((nb, nsub * st_sub), F32),
            pltpu.VMEM((tc * nb, dq), F32),
        ],
        compiler_params=_cparams("arbitrary", "arbitrary"),
        name="s5_scan_rev" if reverse else "s5_scan_fwd",
    )(h_ctx3, h_lat3, a_re, a_im, wb, wc)


def _glu_kernel(x_ref, yf_ref, yr_ref, prm_ref, dv_ref, wa_ref, wb_ref, o_ref, gy_ref, z_ref, *, nj, rc):
    j = pl.program_id(1)
    tm = x_ref.shape[0]
    tn = wa_ref.shape[-1]
    prm = prm_ref[0]

    @pl.when(j == 0)
    def _():
        def body(rows):
            h = _pre_rows(x_ref[rows, :], prm)
            y = h * dv_ref[...] + yf_ref[rows, :] + yr_ref[rows, :]
            gy_ref[rows, :] = jax.nn.gelu(y).astype(BF16)
        _row_chunks(tm, rc, body)

    gy = gy_ref[...]
    z_ref[j] = _dot(gy, wa_ref[...]) * jax.nn.sigmoid(_dot(gy, wb_ref[...]))

    @pl.when(j == nj - 1)
    def _():
        def body(rows):
            ss = jnp.zeros((rc, 1), F32)
            for jj in range(nj):
                zz = z_ref[jj, rows, :]
                ss = ss + jnp.sum(zz * zz, axis=-1, keepdims=True)
            r = lax.rsqrt(ss / (nj * tn) + EPS)
            for jj in range(nj):
                cols = slice(jj * tn, (jj + 1) * tn)
                zz = z_ref[jj, rows, :]
                o_ref[rows, cols] = x_ref[rows, cols] + prm[ROW_GATE:ROW_GATE + 1, cols] * (zz * r * prm[ROW_GPOST:ROW_GPOST + 1, cols])
        _row_chunks(tm, rc, body)


def _glu(x2, yf2, yr2, prm, rows_per_sample, dv, wa, wb, o, *, tm, tn):
    n, d = x2.shape
    tm = min(tm, rows_per_sample)
    tn = _pick_tile(d, tn)
    nj = d // tn
    tiles_per_sample = rows_per_sample // tm
    n_samples = prm.shape[0]
    rc = min(128, tm)
    return pl.pallas_call(
        functools.partial(_glu_kernel, nj=nj, rc=rc),
        grid=(n // tm, nj),
        in_specs=[
            pl.BlockSpec((tm, d), lambda i, j: (i, 0)),
            pl.BlockSpec((tm, d), lambda i, j: (i, 0)),
            pl.BlockSpec((tm, d), lambda i, j: (i, 0)),
            pl.BlockSpec((1, 8, d), lambda i, j: (jnp.minimum(i // tiles_per_sample, n_samples - 1), 0, 0)),
            pl.BlockSpec((None, 1, d), lambda i, j: (o, 0, 0)),
            pl.BlockSpec((None, d, tn), lambda i, j: (o, 0, j)),
            pl.BlockSpec((None, d, tn), lambda i, j: (o, 0, j)),
        ],
        out_specs=pl.BlockSpec((tm, d), lambda i, j: (i, 0)),
        out_shape=jax.ShapeDtypeStruct((n, d), F32),
        scratch_shapes=[pltpu.VMEM((tm, d), BF16), pltpu.VMEM((nj, tm, tn), F32)],
        compiler_params=_cparams("arbitrary", "arbitrary"),
        name="s5_glu",
    )(x2, yf2, yr2, prm, dv, wa, wb)


def _param_pack(m, g, i):
    s, _, d = m.shape
    rows = [m[:, 3 * i], m[:, 3 * i + 1], m[:, 3 * i + 2],
            jnp.broadcast_to(g[2 * i], (s, d)), jnp.broadcast_to(g[2 * i + 1], (s, d))]
    rows += [jnp.zeros((s, d), F32)] * 3
    return jnp.stack(rows, axis=1)


def _s5_discretise(lam_re, lam_im, log_step, b_re, b_im):
    dt = jnp.exp(log_step)[..., None]
    mag = jnp.exp(lam_re * dt)
    lbar_re = mag * jnp.cos(lam_im * dt)
    lbar_im = mag * jnp.sin(lam_im * dt)
    den = lam_re * lam_re + lam_im * lam_im
    nr = lbar_re - 1.0
    fr = (nr * lam_re + lbar_im * lam_im) / den
    fi = (lbar_im * lam_re - nr * lam_im) / den
    bbar_re = fr[..., None] * b_re - fi[..., None] * b_im
    bbar_im = fr[..., None] * b_im + fi[..., None] * b_re
    return lbar_re, lbar_im, bbar_re, bbar_im


def _s5_layout(lbar_re, lbar_im, bbar_re, bbar_im, c_re, c_im, nb, gsub, nsub):
    g, p, h = bbar_re.shape
    nblk = g // gsub
    eye = jnp.eye(gsub, dtype=F32)

    def drive(bb):
        bb = bb.reshape(nblk, gsub, p, h)
        return jnp.einsum('ngph,gk->nghkp', bb, eye).reshape(nblk, gsub * h, gsub * p)

    def read(cc):
        cc = cc.reshape(nblk, gsub, h, p)
        return jnp.einsum('nghp,gk->ngpkh', cc, eye).reshape(nblk, gsub * p, gsub * h)

    wb = jnp.concatenate([drive(bbar_re), drive(bbar_im)], axis=2)
    wc = jnp.concatenate([read(c_re), -read(c_im)], axis=1)
    nq = nblk // nsub
    wb = wb.reshape(nq, nsub, gsub * h, 2 * gsub * p).astype(BF16)
    wc = wc.reshape(nq, nsub, 2 * gsub * p, gsub * h).astype(BF16)
    a_re = jnp.broadcast_to(lbar_re.reshape(nq, 1, nsub * gsub * p), (nq, nb, nsub * gsub * p))
    a_im = jnp.broadcast_to(lbar_im.reshape(nq, 1, nsub * gsub * p), (nq, nb, nsub * gsub * p))
    return a_re, a_im, wb, wc


def kernel(x, c, ctx, c_ctx, w_mod, b_mod, norm_g, ffn_w_gate, ffn_w_up, ffn_w_down, mix_in, conv_w, pool_w, pool_scale, mix_out, s5_lambda_re, s5_lambda_im, s5_log_step, s5_b_re, s5_b_im, s5_c_re, s5_c_im, s5_d, glu_w_a, glu_w_b):
    b, l, d = x.shape
    lc = ctx.shape[1]
    depth = w_mod.shape[0]
    assert b == SUBLANES, "the S5 recurrence keeps the batch on the 8 sublanes"
    assert l % GRID_W == 0 and d % LANES == 0

    wg, wu, wd = ffn_w_gate.astype(BF16), ffn_w_up.astype(BF16), ffn_w_down.astype(BF16)
    w_in, w_out = mix_in.astype(BF16), mix_out.astype(BF16)
    pw = pool_w.astype(BF16)
    wa, wbb = glu_w_a.astype(BF16), glu_w_b.astype(BF16)
    conv_w8 = jnp.pad(conv_w, ((0, 0), (0, 8 - conv_w.shape[1]), (0, 0)))
    pscale = pool_scale[:, None, :]

    c_all = jnp.concatenate([c, c_ctx[None], jnp.zeros((2 * SUBLANES - b - 1, d), F32)], axis=0)
    m_all = _modulation(c_all, w_mod, b_mod)[:, :b + 1].reshape(depth, b + 1, N_MOD, d)

    x_lat = x.reshape(b * l, d)
    x_ctx = ctx.reshape(b * lc, d)
    ffn_lat = functools.partial(_ffn, tm=1024, tf=512)
    ffn_ctx = functools.partial(_ffn, tm=1024, tf=512)

    for layer in range(depth):
        last = layer == depth - 1
        even = layer % 2 == 0
        run_ctx_in = not (last and even)
        ctx_out = not last
        g = norm_g[layer]
        m_lat, m_ctx = m_all[layer, :b], m_all[layer, b:]
        pk_lat = [_param_pack(m_lat, g, i) for i in range(3)]
        pk_ctx = [_param_pack(m_ctx, g, i) for i in range(3)]

        x_lat = ffn_lat(x_lat, pk_lat[0], l, wg, wu, wd, layer, 0)
        if run_ctx_in:
            x_ctx = ffn_ctx(x_ctx, pk_ctx[0], b * lc, wg, wu, wd, layer, 0)

        if even:
            e = layer // 2
            p_lat = _preproj(x_lat, pk_lat[1], l, w_in, e, tm=1024, tn=1024)
            x_lat_new = _evenmix(p_lat, x_lat, pk_lat[1], l, conv_w8, pw, pscale, w_out, e, tm=256, blk=GRID_W)
            if ctx_out:
                p_ctx = _preproj(x_ctx, pk_ctx[1], b * lc, w_in, e, tm=1024, tn=1024)
                x_ctx = _evenmix(p_ctx, x_ctx, pk_ctx[1], lc, conv_w8, pw, pscale, w_out, e, tm=lc, blk=lc)
            x_lat = x_lat_new
        else:
            o = layer // 2
            h_lat = _prenorm(x_lat, pk_lat[1], l, tm=1024)
            h_ctx = _prenorm(x_ctx, pk_ctx[1], b * lc, tm=1024)
            lbr, lbi, bbr, bbi = _s5_discretise(s5_lambda_re[o], s5_lambda_im[o], s5_log_step[o], s5_b_re[o], s5_b_im[o])
            n_groups = lbr.shape[1]
            gsub = SUBLANES
            nsub = min(4, n_groups // gsub)
            ys = []
            for dr, rev in ((0, False), (1, True)):
                a_re, a_im, wb_s5, wc_s5 = _s5_layout(lbr[dr], lbi[dr], bbr[dr], bbi[dr], s5_c_re[o, dr], s5_c_im[o, dr],
                                                      b, gsub, nsub)
                ys.append(_s5_scan(h_ctx.reshape(b, lc, d), h_lat.reshape(b, l, d), a_re, a_im, wb_s5, wc_s5,
                                   reverse=rev, tc=64))
            x_lat = _glu(x_lat, ys[0].reshape(b * l, d), ys[1].reshape(b * l, d), pk_lat[1], l, s5_d[:, None, :], wa, wbb, o,
                         tm=512, tn=512)
            assert not ctx_out, "context output of an S5 layer is not needed by this block"

        x_lat = ffn_lat(x_lat, pk_lat[2], l, wg, wu, wd, layer, 1)
        if ctx_out:
            x_ctx = ffn_ctx(x_ctx, pk_ctx[2], b * lc, wg, wu, wd, layer, 1)

    return x_lat.reshape(b, l, d)
```

```python
import functools

import jax
import jax.numpy as jnp
import numpy as np
from jax import lax
from jax.experimental import pallas as pl
from jax.experimental.pallas import tpu as pltpu

EPS = 1e-6
GRID_W = 64
POOL_WINDOWS = (2, 4, 8, 16)
N_MOD = 9

F32 = jnp.float32
BF16 = jnp.bfloat16

LANES = 128
SUBLANES = 8
VMEM_LIMIT_BYTES = 56 * 1024 * 1024
ROW_SHIFT, ROW_SCALE, ROW_GATE, ROW_GPRE, ROW_GPOST = 0, 1, 2, 3, 4


def _cparams(*sem):
    return pltpu.CompilerParams(dimension_semantics=sem, vmem_limit_bytes=VMEM_LIMIT_BYTES)


def _dot(a, b):
    return jnp.dot(a, b, preferred_element_type=F32)


def _pre_rows(x, prm):
    r = lax.rsqrt(jnp.mean(x * x, axis=-1, keepdims=True) + EPS)
    return (x * r * prm[ROW_GPRE:ROW_GPRE + 1]) * (1.0 + prm[ROW_SCALE:ROW_SCALE + 1]) + prm[ROW_SHIFT:ROW_SHIFT + 1]


def _post_rows(x, y, prm, weight):
    r = lax.rsqrt(jnp.mean(y * y, axis=-1, keepdims=True) + EPS)
    return x + (weight * prm[ROW_GATE:ROW_GATE + 1]) * (y * r * prm[ROW_GPOST:ROW_GPOST + 1])


def _pick_tile(total, target, quantum=LANES):
    if total <= target:
        return total
    best = quantum
    for t in range(quantum, target + 1, quantum):
        if total % t == 0:
            best = t
    assert total % best == 0, (total, target, quantum)
    return best


def _row_chunks(total, chunk, body):
    chunk = min(chunk, total)

    def step(k, carry):
        body(pl.ds(pl.multiple_of(k * chunk, chunk), chunk))
        return carry

    lax.fori_loop(0, total // chunk, step, 0)


def _mod_kernel(c_ref, w_ref, b_ref, o_ref):
    sc = jax.nn.silu(c_ref[...]).astype(BF16)
    o_ref[0] = _dot(sc, w_ref[0].astype(BF16)) + b_ref[0]


def _modulation(c_all, w_mod, b_mod):
    depth, d, nd = w_mod.shape
    rows = c_all.shape[0]
    tn = _pick_tile(nd, 1024)
    return pl.pallas_call(
        _mod_kernel,
        grid=(depth, nd // tn),
        in_specs=[
            pl.BlockSpec((rows, d), lambda l, j: (0, 0)),
            pl.BlockSpec((1, d, tn), lambda l, j: (l, 0, j)),
            pl.BlockSpec((1, 1, tn), lambda l, j: (l, 0, j)),
        ],
        out_specs=pl.BlockSpec((1, rows, tn), lambda l, j: (l, 0, j)),
        out_shape=jax.ShapeDtypeStruct((depth, rows, nd), F32),
        compiler_params=_cparams("arbitrary", "arbitrary"),
        name="modulation",
    )(c_all, w_mod, b_mod.reshape(depth, 1, nd))


def _ffn_kernel(x_ref, prm_ref, wg_ref, wu_ref, wd_ref, o_ref, h_ref, *, nj, rc):
    j = pl.program_id(1)
    tm = x_ref.shape[0]
    prm = prm_ref[0]

    @pl.when(j == 0)
    def _():
        def body(rows):
            h_ref[rows, :] = _pre_rows(x_ref[rows, :], prm).astype(BF16)
            o_ref[rows, :] = jnp.zeros((rows.size, o_ref.shape[1]), F32)
        _row_chunks(tm, rc, body)

    h = h_ref[...]
    ga = _dot(h, wg_ref[...])
    up = _dot(h, wu_ref[...])
    t = (jax.nn.silu(ga) * up).astype(BF16)
    o_ref[...] += _dot(t, wd_ref[...])

    @pl.when(j == nj - 1)
    def _():
        def body(rows):
            o_ref[rows, :] = _post_rows(x_ref[rows, :], o_ref[rows, :], prm, 0.5)
        _row_chunks(tm, rc, body)


def _ffn(x2, prm, rows_per_sample, wg, wu, wd, l, k, *, tm, tf):
    n, d = x2.shape
    f = wg.shape[-1]
    tm = min(tm, rows_per_sample)
    tf = _pick_tile(f, tf)
    nj = f // tf
    tiles_per_sample = max(rows_per_sample // tm, 1)
    n_samples = prm.shape[0]
    kern = functools.partial(_ffn_kernel, nj=nj, rc=128)
    return pl.pallas_call(
        kern,
        grid=(n // tm, nj),
        in_specs=[
            pl.BlockSpec((tm, d), lambda i, j: (i, 0)),
            pl.BlockSpec((1, 8, d), lambda i, j: (jnp.minimum(i // tiles_per_sample, n_samples - 1), 0, 0)),
            pl.BlockSpec((None, None, d, tf), lambda i, j: (l, k, 0, j)),
            pl.BlockSpec((None, None, d, tf), lambda i, j: (l, k, 0, j)),
            pl.BlockSpec((None, None, tf, d), lambda i, j: (l, k, j, 0)),
        ],
        out_specs=pl.BlockSpec((tm, d), lambda i, j: (i, 0)),
        out_shape=jax.ShapeDtypeStruct((n, d), F32),
        scratch_shapes=[pltpu.VMEM((tm, d), BF16)],
        compiler_params=_cparams("arbitrary", "arbitrary"),
        name="ffn",
    )(x2, prm, wg, wu, wd)


def _preproj_kernel(x_ref, prm_ref, w_ref, o_ref, h_ref, *, rc):
    j = pl.program_id(1)
    tm = x_ref.shape[0]
    prm = prm_ref[0]

    @pl.when(j == 0)
    def _():
        def body(rows):
            h_ref[rows, :] = _pre_rows(x_ref[rows, :], prm).astype(BF16)
        _row_chunks(tm, rc, body)

    o_ref[...] = _dot(h_ref[...], w_ref[...])


def _preproj(x2, prm, rows_per_sample, w, e, *, tm, tn):
    n, d = x2.shape
    nc = w.shape[-1]
    tm = min(tm, rows_per_sample)
    tn = _pick_tile(nc, tn)
    tiles_per_sample = max(rows_per_sample // tm, 1)
    n_samples = prm.shape[0]
    return pl.pallas_call(
        functools.partial(_preproj_kernel, rc=128),
        grid=(n // tm, nc // tn),
        in_specs=[
            pl.BlockSpec((tm, d), lambda i, j: (i, 0)),
            pl.BlockSpec((1, 8, d), lambda i, j: (jnp.minimum(i // tiles_per_sample, n_samples - 1), 0, 0)),
            pl.BlockSpec((None, d, tn), lambda i, j: (e, 0, j)),
        ],
        out_specs=pl.BlockSpec((tm, tn), lambda i, j: (i, j)),
        out_shape=jax.ShapeDtypeStruct((n, nc), F32),
        scratch_shapes=[pltpu.VMEM((tm, d), BF16)],
        compiler_params=_cparams("arbitrary", "arbitrary"),
        name="mix_in_proj",
    )(x2, prm, w)


def _evenmix_kernel(p_ref, pprev_ref, pnext_ref, x_ref, prm_ref, cw_ref, pw_ref, ps_ref, wo_ref, o_ref,
                    cv_ref, pa_ref, pb_ref, ycat_ref, *, blk, tiles_per_sample, mix_a, pool_ch):
    i = pl.program_id(0)
    tm = x_ref.shape[0]
    prm = prm_ref[0]
    nblk = tm // blk
    pitch = blk + 2 * SUBLANES
    npad = nblk * pitch + SUBLANES
    t_in_sample = i % tiles_per_sample
    has_prev = (t_in_sample != 0).astype(F32)
    has_next = (t_in_sample != tiles_per_sample - 1).astype(F32)

    c0, c1, c2, c3 = 0, mix_a, 2 * mix_a, 3 * mix_a
    cv = p_ref[:, c1:c2] * p_ref[:, c2:c3]
    cv_ref[SUBLANES:SUBLANES + tm, :] = cv
    cv_ref[0:SUBLANES, :] = pprev_ref[:, c1:c2] * pprev_ref[:, c2:c3] * has_prev
    cv_ref[SUBLANES + tm:2 * SUBLANES + tm, :] = pnext_ref[:, c1:c2] * pnext_ref[:, c2:c3] * has_next
    conv = (cw_ref[0:1, :] * cv_ref[SUBLANES - 1:SUBLANES - 1 + tm, :] + cw_ref[1:2, :] * cv
            + cw_ref[2:3, :] * cv_ref[SUBLANES + 1:SUBLANES + 1 + tm, :])
    ycat_ref[:, 0:mix_a] = (p_ref[:, c0:c1] * conv).astype(BF16)

    pos = lax.broadcasted_iota(jnp.int32, (tm, 1), 0) % blk
    inner = npad - 2 * SUBLANES
    for gi, w in enumerate(POOL_WINDOWS):
        lo, hi = c3 + gi * pool_ch, c3 + (gi + 1) * pool_ch
        pa_ref[...] = jnp.zeros_like(pa_ref)
        for b in range(nblk):
            pa_ref[b * pitch + SUBLANES:b * pitch + SUBLANES + blk, :] = p_ref[b * blk:(b + 1) * blk, lo:hi]
        src, dst = pa_ref, pb_ref
        dst[...] = jnp.zeros_like(dst)
        dst[SUBLANES:SUBLANES + inner, :] = src[SUBLANES - 1:SUBLANES - 1 + inner, :] + src[SUBLANES:SUBLANES + inner, :]
        src, dst = dst, src
        half = 1
        while 4 * half <= w:
            dst[...] = jnp.zeros_like(dst)
            dst[SUBLANES:SUBLANES + inner, :] = (src[SUBLANES - half:SUBLANES - half + inner, :]
                                                 + src[SUBLANES + half:SUBLANES + half + inner, :])
            src, dst = dst, src
            half *= 2
        cnt = (jnp.minimum(pos + w // 2, blk) - jnp.maximum(pos - w // 2, 0)).astype(F32)
        pieces = [src[b * pitch + SUBLANES:b * pitch + SUBLANES + blk, :] for b in range(nblk)]
        total = pieces[0] if nblk == 1 else jnp.concatenate(pieces, axis=0)
        pooled = total / cnt - p_ref[:, lo:hi]
        yb = _dot(pooled.astype(BF16), pw_ref[gi]) * ps_ref[:, gi * pool_ch:(gi + 1) * pool_ch]
        ycat_ref[:, mix_a + gi * pool_ch:mix_a + (gi + 1) * pool_ch] = yb.astype(BF16)

    y = _dot(ycat_ref[...], wo_ref[...])
    o_ref[...] = _post_rows(x_ref[...], y, prm, 1.0)


def _evenmix(p2, x2, prm, rows_per_sample, conv_w8, pool_w, pool_scale, w_out, e, *, tm, blk):
    n, d = x2.shape
    nc = p2.shape[1]
    mix_a = conv_w8.shape[-1]
    pool_ch = pool_w.shape[-1]
    mix_b = pool_ch * len(POOL_WINDOWS)
    tm = min(tm, rows_per_sample)
    tiles_per_sample = rows_per_sample // tm
    n_samples = prm.shape[0]
    nblk = tm // blk
    npad = nblk * (blk + 2 * SUBLANES) + SUBLANES
    tb = tm // SUBLANES
    nb8 = n // SUBLANES
    kern = functools.partial(_evenmix_kernel, blk=blk, tiles_per_sample=tiles_per_sample, mix_a=mix_a, pool_ch=pool_ch)
    return pl.pallas_call(
        kern,
        grid=(n // tm,),
        in_specs=[
            pl.BlockSpec((tm, nc), lambda i: (i, 0)),
            pl.BlockSpec((SUBLANES, nc), lambda i: (jnp.maximum(i * tb - 1, 0), 0)),
            pl.BlockSpec((SUBLANES, nc), lambda i: (jnp.minimum((i + 1) * tb, nb8 - 1), 0)),
            pl.BlockSpec((tm, d), lambda i: (i, 0)),
            pl.BlockSpec((1, 8, d), lambda i: (jnp.minimum(i // tiles_per_sample, n_samples - 1), 0, 0)),
            pl.BlockSpec((None, 8, mix_a), lambda i: (e, 0, 0)),
            pl.BlockSpec((None, len(POOL_WINDOWS), pool_ch, pool_ch), lambda i: (e, 0, 0, 0)),
            pl.BlockSpec((None, 1, mix_b), lambda i: (e, 0, 0)),
            pl.BlockSpec((None, mix_a + mix_b, d), lambda i: (e, 0, 0)),
        ],
        out_specs=pl.BlockSpec((tm, d), lambda i: (i, 0)),
        out_shape=jax.ShapeDtypeStruct((n, d), F32),
        scratch_shapes=[
            pltpu.VMEM((tm + 2 * SUBLANES, mix_a), F32),
            pltpu.VMEM((npad, pool_ch), F32),
            pltpu.VMEM((npad, pool_ch), F32),
            pltpu.VMEM((tm, mix_a + mix_b), BF16),
        ],
        compiler_params=_cparams("arbitrary"),
        name="even_mixer",
    )(p2, p2, p2, x2, prm, conv_w8, pool_w, pool_scale, w_out)


def _prenorm_kernel(x_ref, prm_ref, o_ref, *, rc):
    prm = prm_ref[0]

    def body(rows):
        o_ref[rows, :] = _pre_rows(x_ref[rows, :], prm)
    _row_chunks(x_ref.shape[0], rc, body)


def _prenorm(x2, prm, rows_per_sample, *, tm):
    n, d = x2.shape
    tm = min(tm, rows_per_sample)
    tiles_per_sample = rows_per_sample // tm
    n_samples = prm.shape[0]
    return pl.pallas_call(
        functools.partial(_prenorm_kernel, rc=128),
        grid=(n // tm,),
        in_specs=[
            pl.BlockSpec((tm, d), lambda i: (i, 0)),
            pl.BlockSpec((1, 8, d), lambda i: (jnp.minimum(i // tiles_per_sample, n_samples - 1), 0, 0)),
        ],
        out_specs=pl.BlockSpec((tm, d), lambda i: (i, 0)),
        out_shape=jax.ShapeDtypeStruct((n, d), F32),
        compiler_params=_cparams("arbitrary"),
        name="s5_prenorm",
    )(x2, prm)


def _s5_kernel(hc_ref, hl_ref, are_ref, aim_ref, wb_ref, wc_ref, o_ref,
               slab_ref, htm_ref, bu_ref, st_ref, ytm_ref, *, reverse, n_ctx_chunks, tc, nsub, ch_sub, st_sub):
    c = pl.program_id(1)
    nb = hl_ref.shape[0]
    half = st_sub // 2
    pitch = tc + SUBLANES
    nslab = slab_ref.shape[0]

    @pl.when(c == 0)
    def _():
        st_ref[...] = jnp.zeros_like(st_ref)

    def to_time_major(src_ref):
        for b in range(nb):
            for k in range(nslab):
                slab_ref[k, b * pitch:b * pitch + tc, :] = src_ref[b, :, k * LANES:(k + 1) * LANES]

        def gather(t, carry):
            row = pl.multiple_of(t * nb, nb)
            for k in range(nslab):
                htm_ref[pl.ds(row, nb), k * LANES:(k + 1) * LANES] = slab_ref.at[k][pl.ds(t, nb, stride=pitch), :]
            return carry
        lax.fori_loop(0, tc, gather, 0, unroll=8)

    @pl.when(c < n_ctx_chunks)
    def _():
        to_time_major(hc_ref)

    @pl.when(c >= n_ctx_chunks)
    def _():
        to_time_major(hl_ref)

    for j in range(nsub):
        hj = htm_ref[:, j * ch_sub:(j + 1) * ch_sub].astype(BF16)
        bu_ref[:, j * st_sub:(j + 1) * st_sub] = _dot(hj, wb_ref[0, j])

    def scan_step(s, st):
        t = (tc - 1 - s) if reverse else s
        rows = pl.ds(pl.multiple_of(t * nb, nb), nb)
        new = []
        for j in range(nsub):
            lo = j * st_sub
            sr, si = st[:, lo:lo + half], st[:, lo + half:lo + st_sub]
            ar = are_ref[0, :, j * half:(j + 1) * half]
            ai = aim_ref[0, :, j * half:(j + 1) * half]
            nr = ar * sr - ai * si + bu_ref[rows, lo:lo + half]
            ni = ar * si + ai * sr + bu_ref[rows, lo + half:lo + st_sub]
            new += [nr, ni]
        st = jnp.concatenate(new, axis=1)
        bu_ref[rows, :] = st
        return st

    st_ref[...] = lax.fori_loop(0, tc, scan_step, st_ref[...], unroll=4)

    @pl.when(c >= n_ctx_chunks)
    def _():
        for j in range(nsub):
            sj = bu_ref[:, j * st_sub:(j + 1) * st_sub].astype(BF16)
            ytm_ref[:, j * ch_sub:(j + 1) * ch_sub] = _dot(sj, wc_ref[0, j])
        for k in range(nslab):
            slab_ref[k, 0:tc * nb, :] = ytm_ref[:, k * LANES:(k + 1) * LANES]
        for b in range(nb):
            for k in range(nslab):
                o_ref[b, :, k * LANES:(k + 1) * LANES] = slab_ref.at[k][pl.ds(b, tc, stride=nb), :]


def _s5_scan(h_ctx3, h_lat3, a_re, a_im, wb, wc, *, reverse, tc):
    nb, lc, d = h_ctx3.shape
    l = h_lat3.shape[1]
    nq, nsub, ch_sub, st_sub = wb.shape
    dq = nsub * ch_sub
    tc = min(tc, lc)
    ncc, nlc = lc // tc, l // tc
    pitch = tc + SUBLANES

    def ctx_idx(c):
        cc = jnp.minimum(c, ncc - 1)
        return (ncc - 1 - cc) if reverse else cc

    def lat_idx(c):
        lc_ = jnp.maximum(c - ncc, 0)
        return (nlc - 1 - lc_) if reverse else lc_

    kern = functools.partial(_s5_kernel, reverse=reverse, n_ctx_chunks=ncc, tc=tc, nsub=nsub, ch_sub=ch_sub, st_sub=st_sub)
    return pl.pallas_call(
        kern,
        grid=(nq, ncc + nlc),
        in_specs=[
            pl.BlockSpec((nb, tc, dq), lambda q, c: (0, ctx_idx(c), q)),
            pl.BlockSpec((nb, tc, dq), lambda q, c: (0, lat_idx(c), q)),
            pl.BlockSpec((1, nb, nsub * st_sub // 2), lambda q, c: (q, 0, 0)),
            pl.BlockSpec((1, nb, nsub * st_sub // 2), lambda q, c: (q, 0, 0)),
            pl.BlockSpec((1, nsub, ch_sub, st_sub), lambda q, c: (q, 0, 0, 0)),
            pl.BlockSpec((1, nsub, st_sub, ch_sub), lambda q, c: (q, 0, 0, 0)),
        ],
        out_specs=pl.BlockSpec((nb, tc, dq), lambda q, c: (0, lat_idx(c), q)),
        out_shape=jax.ShapeDtypeStruct((nb, l, d), F32),
        scratch_shapes=[
            pltpu.VMEM((dq // LANES, nb * pitch, LANES), F32),
            pltpu.VMEM((tc * nb, dq), F32),
            pltpu.VMEM((tc * nb, nsub * st_sub), F32),
            pltpu.VMEM((nb, nsub * st_sub), F32),
            pltpu.VMEM((tc * nb, dq), F32),
        ],
        compiler_params=_cparams("arbitrary", "arbitrary"),
        name="s5_scan_rev" if reverse else "s5_scan_fwd",
    )(h_ctx3, h_lat3, a_re, a_im, wb, wc)


def _glu_kernel(x_ref, yf_ref, yr_ref, prm_ref, dv_ref, wa_ref, wb_ref, o_ref, *, rc):
    tm = x_ref.shape[0]
    prm = prm_ref[0]
    for c in range(tm // rc):
        rows = slice(c * rc, (c + 1) * rc)
        x = x_ref[rows, :]
        h = _pre_rows(x, prm)
        y = h * dv_ref[...] + yf_ref[rows, :] + yr_ref[rows, :]
        gy = jax.nn.gelu(y).astype(BF16)
        z = _dot(gy, wa_ref[...]) * jax.nn.sigmoid(_dot(gy, wb_ref[...]))
        o_ref[rows, :] = _post_rows(x, z, prm, 1.0)


def _glu(x2, yf2, yr2, prm, rows_per_sample, dv, wa, wb, o, *, tm, rc):
    n, d = x2.shape
    tm = min(tm, rows_per_sample)
    rc = min(rc, tm)
    tiles_per_sample = rows_per_sample // tm
    n_samples = prm.shape[0]
    return pl.pallas_call(
        functools.partial(_glu_kernel, rc=rc),
        grid=(n // tm,),
        in_specs=[
            pl.BlockSpec((tm, d), lambda i: (i, 0)),
            pl.BlockSpec((tm, d), lambda i: (i, 0)),
            pl.BlockSpec((tm, d), lambda i: (i, 0)),
            pl.BlockSpec((1, 8, d), lambda i: (jnp.minimum(i // tiles_per_sample, n_samples - 1), 0, 0)),
            pl.BlockSpec((None, 1, d), lambda i: (o, 0, 0)),
            pl.BlockSpec((None, d, d), lambda i: (o, 0, 0), pipeline_mode=pl.Buffered(1)),
            pl.BlockSpec((None, d, d), lambda i: (o, 0, 0), pipeline_mode=pl.Buffered(1)),
        ],
        out_specs=pl.BlockSpec((tm, d), lambda i: (i, 0)),
        out_shape=jax.ShapeDtypeStruct((n, d), F32),
        compiler_params=_cparams("arbitrary"),
        name="s5_glu",
    )(x2, yf2, yr2, prm, dv, wa, wb)


def _param_pack(m, g, i):
    s, _, d = m.shape
    rows = [m[:, 3 * i], m[:, 3 * i + 1], m[:, 3 * i + 2],
            jnp.broadcast_to(g[2 * i], (s, d)), jnp.broadcast_to(g[2 * i + 1], (s, d))]
    rows += [jnp.zeros((s, d), F32)] * 3
    return jnp.stack(rows, axis=1)


def _s5_discretise(lam_re, lam_im, log_step, b_re, b_im):
    dt = jnp.exp(log_step)[..., None]
    mag = jnp.exp(lam_re * dt)
    lbar_re = mag * jnp.cos(lam_im * dt)
    lbar_im = mag * jnp.sin(lam_im * dt)
    den = lam_re * lam_re + lam_im * lam_im
    nr = lbar_re - 1.0
    fr = (nr * lam_re + lbar_im * lam_im) / den
    fi = (lbar_im * lam_re - nr * lam_im) / den
    bbar_re = fr[..., None] * b_re - fi[..., None] * b_im
    bbar_im = fr[..., None] * b_im + fi[..., None] * b_re
    return lbar_re, lbar_im, bbar_re, bbar_im


def _s5_layout(lbar_re, lbar_im, bbar_re, bbar_im, c_re, c_im, nb, gsub, nsub):
    g, p, h = bbar_re.shape
    nblk = g // gsub
    eye = jnp.eye(gsub, dtype=F32)

    def drive(bb):
        bb = bb.reshape(nblk, gsub, p, h)
        return jnp.einsum('ngph,gk->nghkp', bb, eye).reshape(nblk, gsub * h, gsub * p)

    def read(cc):
        cc = cc.reshape(nblk, gsub, h, p)
        return jnp.einsum('nghp,gk->ngpkh', cc, eye).reshape(nblk, gsub * p, gsub * h)

    wb = jnp.concatenate([drive(bbar_re), drive(bbar_im)], axis=2)
    wc = jnp.concatenate([read(c_re), -read(c_im)], axis=1)
    nq = nblk // nsub
    wb = wb.reshape(nq, nsub, gsub * h, 2 * gsub * p).astype(BF16)
    wc = wc.reshape(nq, nsub, 2 * gsub * p, gsub * h).astype(BF16)
    a_re = jnp.broadcast_to(lbar_re.reshape(nq, 1, nsub * gsub * p), (nq, nb, nsub * gsub * p))
    a_im = jnp.broadcast_to(lbar_im.reshape(nq, 1, nsub * gsub * p), (nq, nb, nsub * gsub * p))
    return a_re, a_im, wb, wc


def kernel(x, c, ctx, c_ctx, w_mod, b_mod, norm_g, ffn_w_gate, ffn_w_up, ffn_w_down, mix_in, conv_w, pool_w, pool_scale, mix_out, s5_lambda_re, s5_lambda_im, s5_log_step, s5_b_re, s5_b_im, s5_c_re, s5_c_im, s5_d, glu_w_a, glu_w_b):
    b, l, d = x.shape
    lc = ctx.shape[1]
    depth = w_mod.shape[0]
    assert b == SUBLANES, "the S5 recurrence keeps the batch on the 8 sublanes"
    assert l % GRID_W == 0 and d % LANES == 0

    wg, wu, wd = ffn_w_gate.astype(BF16), ffn_w_up.astype(BF16), ffn_w_down.astype(BF16)
    w_in, w_out = mix_in.astype(BF16), mix_out.astype(BF16)
    pw = pool_w.astype(BF16)
    wa, wbb = glu_w_a.astype(BF16), glu_w_b.astype(BF16)
    conv_w8 = jnp.pad(conv_w, ((0, 0), (0, 8 - conv_w.shape[1]), (0, 0)))
    pscale = pool_scale[:, None, :]

    c_all = jnp.concatenate([c, c_ctx[None], jnp.zeros((2 * SUBLANES - b - 1, d), F32)], axis=0)
    m_all = _modulation(c_all, w_mod, b_mod)[:, :b + 1].reshape(depth, b + 1, N_MOD, d)

    x_lat = x.reshape(b * l, d)
    x_ctx = ctx.reshape(b * lc, d)
    ffn_lat = functools.partial(_ffn, tm=1024, tf=512)
    ffn_ctx = functools.partial(_ffn, tm=1024, tf=512)

    for layer in range(depth):
        last = layer == depth - 1
        even = layer % 2 == 0
        run_ctx_in = not (last and even)
        ctx_out = not last
        g = norm_g[layer]
        m_lat, m_ctx = m_all[layer, :b], m_all[layer, b:]
        pk_lat = [_param_pack(m_lat, g, i) for i in range(3)]
        pk_ctx = [_param_pack(m_ctx, g, i) for i in range(3)]

        x_lat = ffn_lat(x_lat, pk_lat[0], l, wg, wu, wd, layer, 0)
        if run_ctx_in:
            x_ctx = ffn_ctx(x_ctx, pk_ctx[0], b * lc, wg, wu, wd, layer, 0)

        if even:
            e = layer // 2
            p_lat = _preproj(x_lat, pk_lat[1], l, w_in, e, tm=1024, tn=1024)
            x_lat_new = _evenmix(p_lat, x_lat, pk_lat[1], l, conv_w8, pw, pscale, w_out, e, tm=256, blk=GRID_W)
            if ctx_out:
                p_ctx = _preproj(x_ctx, pk_ctx[1], b * lc, w_in, e, tm=1024, tn=1024)
                x_ctx = _evenmix(p_ctx, x_ctx, pk_ctx[1], lc, conv_w8, pw, pscale, w_out, e, tm=lc, blk=lc)
            x_lat = x_lat_new
        else:
            o = layer // 2
            h_lat = _prenorm(x_lat, pk_lat[1], l, tm=1024)
            h_ctx = _prenorm(x_ctx, pk_ctx[1], b * lc, tm=1024)
            lbr, lbi, bbr, bbi = _s5_discretise(s5_lambda_re[o], s5_lambda_im[o], s5_log_step[o], s5_b_re[o], s5_b_im[o])
            n_groups = lbr.shape[1]
            gsub = SUBLANES
            nsub = min(4, n_groups // gsub)
            ys = []
            for dr, rev in ((0, False), (1, True)):
                a_re, a_im, wb_s5, wc_s5 = _s5_layout(lbr[dr], lbi[dr], bbr[dr], bbi[dr], s5_c_re[o, dr], s5_c_im[o, dr],
                                                      b, gsub, nsub)
                ys.append(_s5_scan(h_ctx.reshape(b, lc, d), h_lat.reshape(b, l, d), a_re, a_im, wb_s5, wc_s5,
                                   reverse=rev, tc=64))
            x_lat = _glu(x_lat, ys[0].reshape(b * l, d), ys[1].reshape(b * l, d), pk_lat[1], l, s5_d[:, None, :], wa, wbb, o,
                         tm=512, rc=256)
            assert not ctx_out, "context output of an S5 layer is not needed by this block"

        x_lat = ffn_lat(x_lat, pk_lat[2], l, wg, wu, wd, layer, 1)
        if ctx_out:
            x_ctx = ffn_ctx(x_ctx, pk_ctx[2], b * lc, wg, wu, wd, layer, 1)

    return x_lat.reshape(b, l, d)
```

```python
import functools

import jax
import jax.numpy as jnp
import numpy as np
from jax import lax
from jax.experimental import pallas as pl
from jax.experimental.pallas import tpu as pltpu

EPS = 1e-6
GRID_W = 64
POOL_WINDOWS = (2, 4, 8, 16)
N_MOD = 9

F32 = jnp.float32
BF16 = jnp.bfloat16

LANES = 128
SUBLANES = 8
VMEM_LIMIT_BYTES = 56 * 1024 * 1024
ROW_SHIFT, ROW_SCALE, ROW_GATE, ROW_GPRE, ROW_GPOST = 0, 1, 2, 3, 4


def _cparams(*sem, flags=None):
    return pltpu.CompilerParams(dimension_semantics=sem, vmem_limit_bytes=VMEM_LIMIT_BYTES, flags=flags)


def _dot(a, b):
    return jnp.dot(a, b, preferred_element_type=F32)


def _pre_rows(x, prm):
    r = lax.rsqrt(jnp.mean(x * x, axis=-1, keepdims=True) + EPS)
    return (x * r * prm[ROW_GPRE:ROW_GPRE + 1]) * (1.0 + prm[ROW_SCALE:ROW_SCALE + 1]) + prm[ROW_SHIFT:ROW_SHIFT + 1]


def _post_rows(x, y, prm, weight):
    r = lax.rsqrt(jnp.mean(y * y, axis=-1, keepdims=True) + EPS)
    return x + (weight * prm[ROW_GATE:ROW_GATE + 1]) * (y * r * prm[ROW_GPOST:ROW_GPOST + 1])


def _pick_tile(total, target, quantum=LANES):
    if total <= target:
        return total
    best = quantum
    for t in range(quantum, target + 1, quantum):
        if total % t == 0:
            best = t
    assert total % best == 0, (total, target, quantum)
    return best


def _row_chunks(total, chunk, body, unroll=1):
    chunk = min(chunk, total)

    def step(k, carry):
        body(pl.ds(pl.multiple_of(k * chunk, chunk), chunk))
        return carry

    lax.fori_loop(0, total // chunk, step, 0, unroll=unroll)


def _mod_kernel(c_ref, w_ref, b_ref, o_ref):
    sc = jax.nn.silu(c_ref[...]).astype(BF16)
    o_ref[0] = _dot(sc, w_ref[0].astype(BF16)) + b_ref[0]


def _modulation(c_all, w_mod, b_mod):
    depth, d, nd = w_mod.shape
    rows = c_all.shape[0]
    tn = _pick_tile(nd, 1024)
    return pl.pallas_call(
        _mod_kernel,
        grid=(depth, nd // tn),
        in_specs=[
            pl.BlockSpec((rows, d), lambda l, j: (0, 0)),
            pl.BlockSpec((1, d, tn), lambda l, j: (l, 0, j)),
            pl.BlockSpec((1, 1, tn), lambda l, j: (l, 0, j)),
        ],
        out_specs=pl.BlockSpec((1, rows, tn), lambda l, j: (l, 0, j)),
        out_shape=jax.ShapeDtypeStruct((depth, rows, nd), F32),
        compiler_params=_cparams("arbitrary", "arbitrary"),
        name="modulation",
    )(c_all, w_mod, b_mod.reshape(depth, 1, nd))


def _ffn_kernel(x_ref, prm_ref, wg_ref, wu_ref, wd_ref, o_ref, h_ref, *, nj, rc):
    j = pl.program_id(1)
    tm = x_ref.shape[0]
    prm = prm_ref[0]

    @pl.when(j == 0)
    def _():
        def body(rows):
            h_ref[rows, :] = _pre_rows(x_ref[rows, :], prm).astype(BF16)
            o_ref[rows, :] = jnp.zeros((rows.size, o_ref.shape[1]), F32)
        _row_chunks(tm, rc, body)

    h = h_ref[...]
    ga = _dot(h, wg_ref[...])
    up = _dot(h, wu_ref[...])
    t = (jax.nn.silu(ga) * up).astype(BF16)
    o_ref[...] += _dot(t, wd_ref[...])

    @pl.when(j == nj - 1)
    def _():
        def body(rows):
            o_ref[rows, :] = _post_rows(x_ref[rows, :], o_ref[rows, :], prm, 0.5)
        _row_chunks(tm, rc, body)


def _ffn(x2, prm, rows_per_sample, wg, wu, wd, l, k, *, tm, tf):
    n, d = x2.shape
    f = wg.shape[-1]
    tm = min(tm, rows_per_sample)
    tf = _pick_tile(f, tf)
    nj = f // tf
    tiles_per_sample = max(rows_per_sample // tm, 1)
    n_samples = prm.shape[0]
    kern = functools.partial(_ffn_kernel, nj=nj, rc=128)
    return pl.pallas_call(
        kern,
        grid=(n // tm, nj),
        in_specs=[
            pl.BlockSpec((tm, d), lambda i, j: (i, 0)),
            pl.BlockSpec((1, 8, d), lambda i, j: (jnp.minimum(i // tiles_per_sample, n_samples - 1), 0, 0)),
            pl.BlockSpec((None, None, d, tf), lambda i, j: (l, k, 0, j)),
            pl.BlockSpec((None, None, d, tf), lambda i, j: (l, k, 0, j)),
            pl.BlockSpec((None, None, tf, d), lambda i, j: (l, k, j, 0)),
        ],
        out_specs=pl.BlockSpec((tm, d), lambda i, j: (i, 0)),
        out_shape=jax.ShapeDtypeStruct((n, d), F32),
        scratch_shapes=[pltpu.VMEM((tm, d), BF16)],
        compiler_params=_cparams("arbitrary", "arbitrary"),
        name="ffn",
    )(x2, prm, wg, wu, wd)


def _preproj_kernel(x_ref, prm_ref, w_ref, o_ref, h_ref, *, rc):
    j = pl.program_id(1)
    tm = x_ref.shape[0]
    prm = prm_ref[0]

    @pl.when(j == 0)
    def _():
        def body(rows):
            h_ref[rows, :] = _pre_rows(x_ref[rows, :], prm).astype(BF16)
        _row_chunks(tm, rc, body)

    o_ref[...] = _dot(h_ref[...], w_ref[...])


def _preproj(x2, prm, rows_per_sample, w, e, *, tm, tn):
    n, d = x2.shape
    nc = w.shape[-1]
    tm = min(tm, rows_per_sample)
    tn = _pick_tile(nc, tn)
    tiles_per_sample = max(rows_per_sample // tm, 1)
    n_samples = prm.shape[0]
    return pl.pallas_call(
        functools.partial(_preproj_kernel, rc=128),
        grid=(n // tm, nc // tn),
        in_specs=[
            pl.BlockSpec((tm, d), lambda i, j: (i, 0)),
            pl.BlockSpec((1, 8, d), lambda i, j: (jnp.minimum(i // tiles_per_sample, n_samples - 1), 0, 0)),
            pl.BlockSpec((None, d, tn), lambda i, j: (e, 0, j)),
        ],
        out_specs=pl.BlockSpec((tm, tn), lambda i, j: (i, j)),
        out_shape=jax.ShapeDtypeStruct((n, nc), F32),
        scratch_shapes=[pltpu.VMEM((tm, d), BF16)],
        compiler_params=_cparams("arbitrary", "arbitrary"),
        name="mix_in_proj",
    )(x2, prm, w)


def _evenmix_kernel(p_ref, pprev_ref, pnext_ref, x_ref, prm_ref, cw_ref, pw_ref, ps_ref, wo_ref, o_ref,
                    cv_ref, pa_ref, pb_ref, ycat_ref, *, blk, tiles_per_sample, mix_a, pool_ch):
    i = pl.program_id(0)
    tm = x_ref.shape[0]
    prm = prm_ref[0]
    nblk = tm // blk
    pitch = blk + 2 * SUBLANES
    npad = nblk * pitch + SUBLANES
    t_in_sample = i % tiles_per_sample
    has_prev = (t_in_sample != 0).astype(F32)
    has_next = (t_in_sample != tiles_per_sample - 1).astype(F32)

    c0, c1, c2, c3 = 0, mix_a, 2 * mix_a, 3 * mix_a
    cv = p_ref[:, c1:c2] * p_ref[:, c2:c3]
    cv_ref[SUBLANES:SUBLANES + tm, :] = cv
    cv_ref[0:SUBLANES, :] = pprev_ref[:, c1:c2] * pprev_ref[:, c2:c3] * has_prev
    cv_ref[SUBLANES + tm:2 * SUBLANES + tm, :] = pnext_ref[:, c1:c2] * pnext_ref[:, c2:c3] * has_next
    conv = (cw_ref[0:1, :] * cv_ref[SUBLANES - 1:SUBLANES - 1 + tm, :] + cw_ref[1:2, :] * cv
            + cw_ref[2:3, :] * cv_ref[SUBLANES + 1:SUBLANES + 1 + tm, :])
    ycat_ref[:, 0:mix_a] = (p_ref[:, c0:c1] * conv).astype(BF16)

    pos = lax.broadcasted_iota(jnp.int32, (tm, 1), 0) % blk
    inner = npad - 2 * SUBLANES
    for gi, w in enumerate(POOL_WINDOWS):
        lo, hi = c3 + gi * pool_ch, c3 + (gi + 1) * pool_ch
        pa_ref[...] = jnp.zeros_like(pa_ref)
        for b in range(nblk):
            pa_ref[b * pitch + SUBLANES:b * pitch + SUBLANES + blk, :] = p_ref[b * blk:(b + 1) * blk, lo:hi]
        src, dst = pa_ref, pb_ref
        dst[...] = jnp.zeros_like(dst)
        dst[SUBLANES:SUBLANES + inner, :] = src[SUBLANES - 1:SUBLANES - 1 + inner, :] + src[SUBLANES:SUBLANES + inner, :]
        src, dst = dst, src
        half = 1
        while 4 * half <= w:
            dst[...] = jnp.zeros_like(dst)
            dst[SUBLANES:SUBLANES + inner, :] = (src[SUBLANES - half:SUBLANES - half + inner, :]
                                                 + src[SUBLANES + half:SUBLANES + half + inner, :])
            src, dst = dst, src
            half *= 2
        cnt = (jnp.minimum(pos + w // 2, blk) - jnp.maximum(pos - w // 2, 0)).astype(F32)
        pieces = [src[b * pitch + SUBLANES:b * pitch + SUBLANES + blk, :] for b in range(nblk)]
        total = pieces[0] if nblk == 1 else jnp.concatenate(pieces, axis=0)
        pooled = total / cnt - p_ref[:, lo:hi]
        yb = _dot(pooled.astype(BF16), pw_ref[gi]) * ps_ref[:, gi * pool_ch:(gi + 1) * pool_ch]
        ycat_ref[:, mix_a + gi * pool_ch:mix_a + (gi + 1) * pool_ch] = yb.astype(BF16)

    y = _dot(ycat_ref[...], wo_ref[...])
    o_ref[...] = _post_rows(x_ref[...], y, prm, 1.0)


def _evenmix(p2, x2, prm, rows_per_sample, conv_w8, pool_w, pool_scale, w_out, e, *, tm, blk):
    n, d = x2.shape
    nc = p2.shape[1]
    mix_a = conv_w8.shape[-1]
    pool_ch = pool_w.shape[-1]
    mix_b = pool_ch * len(POOL_WINDOWS)
    tm = min(tm, rows_per_sample)
    tiles_per_sample = rows_per_sample // tm
    n_samples = prm.shape[0]
    nblk = tm // blk
    npad = nblk * (blk + 2 * SUBLANES) + SUBLANES
    tb = tm // SUBLANES
    nb8 = n // SUBLANES
    kern = functools.partial(_evenmix_kernel, blk=blk, tiles_per_sample=tiles_per_sample, mix_a=mix_a, pool_ch=pool_ch)
    return pl.pallas_call(
        kern,
        grid=(n // tm,),
        in_specs=[
            pl.BlockSpec((tm, nc), lambda i: (i, 0)),
            pl.BlockSpec((SUBLANES, nc), lambda i: (jnp.maximum(i * tb - 1, 0), 0)),
            pl.BlockSpec((SUBLANES, nc), lambda i: (jnp.minimum((i + 1) * tb, nb8 - 1), 0)),
            pl.BlockSpec((tm, d), lambda i: (i, 0)),
            pl.BlockSpec((1, 8, d), lambda i: (jnp.minimum(i // tiles_per_sample, n_samples - 1), 0, 0)),
            pl.BlockSpec((None, 8, mix_a), lambda i: (e, 0, 0)),
            pl.BlockSpec((None, len(POOL_WINDOWS), pool_ch, pool_ch), lambda i: (e, 0, 0, 0)),
            pl.BlockSpec((None, 1, mix_b), lambda i: (e, 0, 0)),
            pl.BlockSpec((None, mix_a + mix_b, d), lambda i: (e, 0, 0)),
        ],
        out_specs=pl.BlockSpec((tm, d), lambda i: (i, 0)),
        out_shape=jax.ShapeDtypeStruct((n, d), F32),
        scratch_shapes=[
            pltpu.VMEM((tm + 2 * SUBLANES, mix_a), F32),
            pltpu.VMEM((npad, pool_ch), F32),
            pltpu.VMEM((npad, pool_ch), F32),
            pltpu.VMEM((tm, mix_a + mix_b), BF16),
        ],
        compiler_params=_cparams("arbitrary"),
        name="even_mixer",
    )(p2, p2, p2, x2, prm, conv_w8, pool_w, pool_scale, w_out)


def _prenorm_kernel(x_ref, prm_ref, o_ref, *, rc):
    prm = prm_ref[0]

    def body(rows):
        o_ref[rows, :] = _pre_rows(x_ref[rows, :], prm)
    _row_chunks(x_ref.shape[0], rc, body)


def _prenorm(x2, prm, rows_per_sample, *, tm):
    n, d = x2.shape
    tm = min(tm, rows_per_sample)
    tiles_per_sample = rows_per_sample // tm
    n_samples = prm.shape[0]
    return pl.pallas_call(
        functools.partial(_prenorm_kernel, rc=128),
        grid=(n // tm,),
        in_specs=[
            pl.BlockSpec((tm, d), lambda i: (i, 0)),
            pl.BlockSpec((1, 8, d), lambda i: (jnp.minimum(i // tiles_per_sample, n_samples - 1), 0, 0)),
        ],
        out_specs=pl.BlockSpec((tm, d), lambda i: (i, 0)),
        out_shape=jax.ShapeDtypeStruct((n, d), F32),
        compiler_params=_cparams("arbitrary"),
        name="s5_prenorm",
    )(x2, prm)


def _s5_drive_pieces(j, hc_ref, hl_ref, use_ctx, wb_ref, slab_ref, htm_ref, hbf_ref, buf_ref, *, tc, ch_sub, st_sub, npiece):
    nb = hl_ref.shape[0]
    pitch = tc + SUBLANES
    ch = slice(j * ch_sub, (j + 1) * ch_sub)

    def relayout():
        for k in range(j * ch_sub // LANES, (j + 1) * ch_sub // LANES):
            cols = slice(k * LANES, (k + 1) * LANES)
            for b in range(nb):
                slab_ref[k, b * pitch:b * pitch + tc, :] = jnp.where(use_ctx, hc_ref[b, :, cols], hl_ref[b, :, cols])
            for t in range(tc):
                htm_ref[t * nb:(t + 1) * nb, cols] = slab_ref.at[k][pl.ds(t, nb, stride=pitch), :]
        hbf_ref[:, ch] = htm_ref[:, ch].astype(BF16)

    def matmul(p):
        w = st_sub // npiece
        cols = slice(j * st_sub + p * w, j * st_sub + (p + 1) * w)
        buf_ref[:, cols] = _dot(hbf_ref[:, ch], wb_ref[0, j, :, p * w:(p + 1) * w])

    return [relayout] + [functools.partial(matmul, p) for p in range(npiece)]


def _s5_readout_pieces(j, wc_ref, buf_ref, ytm_ref, slab_ref, o_ref, *, tc, ch_sub, st_sub, npiece):
    nb = o_ref.shape[0]
    ch = slice(j * ch_sub, (j + 1) * ch_sub)

    def matmul(p):
        m = tc * nb // npiece
        rows = slice(p * m, (p + 1) * m)
        ytm_ref[rows, ch] = _dot(buf_ref[rows, j * st_sub:(j + 1) * st_sub].astype(BF16), wc_ref[0, j])

    def relayout():
        for k in range(j * ch_sub // LANES, (j + 1) * ch_sub // LANES):
            cols = slice(k * LANES, (k + 1) * LANES)
            slab_ref[k, :, :] = ytm_ref[:, cols]
            for b in range(nb):
                o_ref[b, :, cols] = slab_ref.at[k][pl.ds(b, tc, stride=nb), :]

    return [functools.partial(matmul, p) for p in range(npiece)] + [relayout]


def _s5_scan_steps(j, sr, si, ar, ai, steps, buf_ref, *, reverse, tc, st_sub, nb):
    half = st_sub // 2
    lo = j * st_sub
    for s in steps:
        t = (tc - 1 - s) if reverse else s
        rows = slice(t * nb, (t + 1) * nb)
        nr = ar * sr - ai * si + buf_ref[rows, lo:lo + half]
        ni = ar * si + ai * sr + buf_ref[rows, lo + half:lo + st_sub]
        buf_ref[rows, lo:lo + half] = nr
        buf_ref[rows, lo + half:lo + st_sub] = ni
        sr, si = nr, ni
    return sr, si


def _s5_kernel(hc_ref, hl_ref, are_ref, aim_ref, wb_ref, wc_ref, o_ref,
               slab_in_ref, slab_out_ref, htm_ref, hbf_ref, ytm_ref, buf0_ref, buf1_ref, buf2_ref, st_ref,
               *, reverse, n_ctx_chunks, tc, nsub, ch_sub, st_sub):
    s = pl.program_id(1)
    nb = st_ref.shape[0]
    half = st_sub // 2
    bufs = (buf0_ref, buf1_ref, buf2_ref)

    @pl.when(s == 0)
    def _():
        for r in bufs:
            r[...] = jnp.zeros_like(r)
        st_ref[...] = jnp.zeros_like(st_ref)

    use_ctx = s < n_ctx_chunks
    for r in range(3):
        @pl.when(s % 3 == r)
        def _(r=r):
            drive_buf, scan_buf, read_buf = bufs[r], bufs[(r + 2) % 3], bufs[(r + 1) % 3]
            scan_kw = dict(reverse=reverse, tc=tc, st_sub=st_sub, nb=nb)
            for j in range(nsub):
                re_cols = slice(2 * j * half, (2 * j + 1) * half)
                im_cols = slice((2 * j + 1) * half, (2 * j + 2) * half)
                ar = are_ref[0, :, j * half:(j + 1) * half]
                ai = aim_ref[0, :, j * half:(j + 1) * half]
                sr, si = st_ref[:, re_cols], st_ref[:, im_cols]
                dp = _s5_drive_pieces(j, hc_ref, hl_ref, use_ctx, wb_ref, slab_in_ref, htm_ref, hbf_ref, drive_buf,
                                      tc=tc, ch_sub=ch_sub, st_sub=st_sub, npiece=4)
                rp = _s5_readout_pieces(j, wc_ref, read_buf, ytm_ref, slab_out_ref, o_ref,
                                        tc=tc, ch_sub=ch_sub, st_sub=st_sub, npiece=2)
                pieces = [dp[0], rp[0], dp[1], dp[2], rp[1], dp[3], dp[4], rp[2]]
                per = tc // len(pieces)
                for g, piece in enumerate(pieces):
                    last = tc if g == len(pieces) - 1 else (g + 1) * per
                    sr, si = _s5_scan_steps(j, sr, si, ar, ai, range(g * per, last), scan_buf, **scan_kw)
                    piece()
                st_ref[:, re_cols] = sr
                st_ref[:, im_cols] = si


def _s5_scan(h_ctx3, h_lat3, a_re, a_im, wb, wc, *, reverse, tc):
    nb, lc, d = h_ctx3.shape
    l = h_lat3.shape[1]
    nq, nsub, ch_sub, st_sub = wb.shape
    dq = nsub * ch_sub
    tc = min(tc, lc)
    ncc, nlc = lc // tc, l // tc
    pitch = tc + SUBLANES

    n_chunks = ncc + nlc

    def ctx_idx(k):
        cc = jnp.clip(k, 0, ncc - 1)
        return (ncc - 1 - cc) if reverse else cc

    def lat_idx(k):
        lc_ = jnp.clip(k - ncc, 0, nlc - 1)
        return (nlc - 1 - lc_) if reverse else lc_

    kern = functools.partial(_s5_kernel, reverse=reverse, n_ctx_chunks=ncc, tc=tc, nsub=nsub, ch_sub=ch_sub, st_sub=st_sub)
    drive_buf = pltpu.VMEM((tc * nb, nsub * st_sub), F32)
    return pl.pallas_call(
        kern,
        grid=(nq, n_chunks + 2),
        in_specs=[
            pl.BlockSpec((nb, tc, dq), lambda q, s: (0, ctx_idx(jnp.minimum(s, n_chunks - 1)), q)),
            pl.BlockSpec((nb, tc, dq), lambda q, s: (0, lat_idx(jnp.minimum(s, n_chunks - 1)), q)),
            pl.BlockSpec((1, nb, nsub * st_sub // 2), lambda q, s: (q, 0, 0)),
            pl.BlockSpec((1, nb, nsub * st_sub // 2), lambda q, s: (q, 0, 0)),
            pl.BlockSpec((1, nsub, ch_sub, st_sub), lambda q, s: (q, 0, 0, 0)),
            pl.BlockSpec((1, nsub, st_sub, ch_sub), lambda q, s: (q, 0, 0, 0)),
        ],
        out_specs=pl.BlockSpec((nb, tc, dq), lambda q, s: (0, lat_idx(s - 2), q)),
        out_shape=jax.ShapeDtypeStruct((nb, l, d), F32),
        scratch_shapes=[
            pltpu.VMEM((dq // LANES, nb * pitch, LANES), F32),
            pltpu.VMEM((dq // LANES, nb * tc, LANES), F32),
            pltpu.VMEM((tc * nb, dq), F32),
            pltpu.VMEM((tc * nb, dq), BF16),
            pltpu.VMEM((tc * nb, dq), F32),
            drive_buf, drive_buf, drive_buf,
            pltpu.VMEM((nb, nsub * st_sub), F32),
        ],
        compiler_params=_cparams("arbitrary", "arbitrary"),
        name="s5_scan_rev" if reverse else "s5_scan_fwd",
    )(h_ctx3, h_lat3, a_re, a_im, wb, wc)


def _glu_kernel(x_ref, yf_ref, yr_ref, prm_ref, dv_ref, wa_ref, wb_ref, o_ref, *, rc):
    tm = x_ref.shape[0]
    prm = prm_ref[0]
    for c in range(tm // rc):
        rows = slice(c * rc, (c + 1) * rc)
        x = x_ref[rows, :]
        h = _pre_rows(x, prm)
        y = h * dv_ref[...] + yf_ref[rows, :] + yr_ref[rows, :]
        gy = jax.nn.gelu(y).astype(BF16)
        z = _dot(gy, wa_ref[...]) * jax.nn.sigmoid(_dot(gy, wb_ref[...]))
        o_ref[rows, :] = _post_rows(x, z, prm, 1.0)


def _glu(x2, yf2, yr2, prm, rows_per_sample, dv, wa, wb, o, *, tm, rc):
    n, d = x2.shape
    tm = min(tm, rows_per_sample)
    rc = min(rc, tm)
    tiles_per_sample = rows_per_sample // tm
    n_samples = prm.shape[0]
    return pl.pallas_call(
        functools.partial(_glu_kernel, rc=rc),
        grid=(n // tm,),
        in_specs=[
            pl.BlockSpec((tm, d), lambda i: (i, 0)),
            pl.BlockSpec((tm, d), lambda i: (i, 0)),
            pl.BlockSpec((tm, d), lambda i: (i, 0)),
            pl.BlockSpec((1, 8, d), lambda i: (jnp.minimum(i // tiles_per_sample, n_samples - 1), 0, 0)),
            pl.BlockSpec((None, 1, d), lambda i: (o, 0, 0)),
            pl.BlockSpec((None, d, d), lambda i: (o, 0, 0), pipeline_mode=pl.Buffered(1)),
            pl.BlockSpec((None, d, d), lambda i: (o, 0, 0), pipeline_mode=pl.Buffered(1)),
        ],
        out_specs=pl.BlockSpec((tm, d), lambda i: (i, 0)),
        out_shape=jax.ShapeDtypeStruct((n, d), F32),
        compiler_params=_cparams("arbitrary"),
        name="s5_glu",
    )(x2, yf2, yr2, prm, dv, wa, wb)


def _param_pack(m, g, i):
    s, _, d = m.shape
    rows = [m[:, 3 * i], m[:, 3 * i + 1], m[:, 3 * i + 2],
            jnp.broadcast_to(g[2 * i], (s, d)), jnp.broadcast_to(g[2 * i + 1], (s, d))]
    rows += [jnp.zeros((s, d), F32)] * 3
    return jnp.stack(rows, axis=1)


def _s5_discretise(lam_re, lam_im, log_step, b_re, b_im):
    dt = jnp.exp(log_step)[..., None]
    mag = jnp.exp(lam_re * dt)
    lbar_re = mag * jnp.cos(lam_im * dt)
    lbar_im = mag * jnp.sin(lam_im * dt)
    den = lam_re * lam_re + lam_im * lam_im
    nr = lbar_re - 1.0
    fr = (nr * lam_re + lbar_im * lam_im) / den
    fi = (lbar_im * lam_re - nr * lam_im) / den
    bbar_re = fr[..., None] * b_re - fi[..., None] * b_im
    bbar_im = fr[..., None] * b_im + fi[..., None] * b_re
    return lbar_re, lbar_im, bbar_re, bbar_im


def _s5_layout(lbar_re, lbar_im, bbar_re, bbar_im, c_re, c_im, nb, gsub, nsub):
    g, p, h = bbar_re.shape
    nblk = g // gsub
    eye = jnp.eye(gsub, dtype=F32)

    def drive(bb):
        bb = bb.reshape(nblk, gsub, p, h)
        return jnp.einsum('ngph,gk->nghkp', bb, eye).reshape(nblk, gsub * h, gsub * p)

    def read(cc):
        cc = cc.reshape(nblk, gsub, h, p)
        return jnp.einsum('nghp,gk->ngpkh', cc, eye).reshape(nblk, gsub * p, gsub * h)

    wb = jnp.concatenate([drive(bbar_re), drive(bbar_im)], axis=2)
    wc = jnp.concatenate([read(c_re), -read(c_im)], axis=1)
    nq = nblk // nsub
    wb = wb.reshape(nq, nsub, gsub * h, 2 * gsub * p).astype(BF16)
    wc = wc.reshape(nq, nsub, 2 * gsub * p, gsub * h).astype(BF16)
    a_re = jnp.broadcast_to(lbar_re.reshape(nq, 1, nsub * gsub * p), (nq, nb, nsub * gsub * p))
    a_im = jnp.broadcast_to(lbar_im.reshape(nq, 1, nsub * gsub * p), (nq, nb, nsub * gsub * p))
    return a_re, a_im, wb, wc


def kernel(x, c, ctx, c_ctx, w_mod, b_mod, norm_g, ffn_w_gate, ffn_w_up, ffn_w_down, mix_in, conv_w, pool_w, pool_scale, mix_out, s5_lambda_re, s5_lambda_im, s5_log_step, s5_b_re, s5_b_im, s5_c_re, s5_c_im, s5_d, glu_w_a, glu_w_b):
    b, l, d = x.shape
    lc = ctx.shape[1]
    depth = w_mod.shape[0]
    assert b == SUBLANES, "the S5 recurrence keeps the batch on the 8 sublanes"
    assert l % GRID_W == 0 and d % LANES == 0

    wg, wu, wd = ffn_w_gate.astype(BF16), ffn_w_up.astype(BF16), ffn_w_down.astype(BF16)
    w_in, w_out = mix_in.astype(BF16), mix_out.astype(BF16)
    pw = pool_w.astype(BF16)
    wa, wbb = glu_w_a.astype(BF16), glu_w_b.astype(BF16)
    conv_w8 = jnp.pad(conv_w, ((0, 0), (0, 8 - conv_w.shape[1]), (0, 0)))
    pscale = pool_scale[:, None, :]

    c_all = jnp.concatenate([c, c_ctx[None], jnp.zeros((2 * SUBLANES - b - 1, d), F32)], axis=0)
    m_all = _modulation(c_all, w_mod, b_mod)[:, :b + 1].reshape(depth, b + 1, N_MOD, d)

    x_lat = x.reshape(b * l, d)
    x_ctx = ctx.reshape(b * lc, d)
    ffn_lat = functools.partial(_ffn, tm=1024, tf=512)
    ffn_ctx = functools.partial(_ffn, tm=1024, tf=512)

    for layer in range(depth):
        last = layer == depth - 1
        even = layer % 2 == 0
        run_ctx_in = not (last and even)
        ctx_out = not last
        g = norm_g[layer]
        m_lat, m_ctx = m_all[layer, :b], m_all[layer, b:]
        pk_lat = [_param_pack(m_lat, g, i) for i in range(3)]
        pk_ctx = [_param_pack(m_ctx, g, i) for i in range(3)]

        x_lat = ffn_lat(x_lat, pk_lat[0], l, wg, wu, wd, layer, 0)
        if run_ctx_in:
            x_ctx = ffn_ctx(x_ctx, pk_ctx[0], b * lc, wg, wu, wd, layer, 0)

        if even:
            e = layer // 2
            p_lat = _preproj(x_lat, pk_lat[1], l, w_in, e, tm=1024, tn=1024)
            x_lat_new = _evenmix(p_lat, x_lat, pk_lat[1], l, conv_w8, pw, pscale, w_out, e, tm=256, blk=GRID_W)
            if ctx_out:
                p_ctx = _preproj(x_ctx, pk_ctx[1], b * lc, w_in, e, tm=1024, tn=1024)
                x_ctx = _evenmix(p_ctx, x_ctx, pk_ctx[1], lc, conv_w8, pw, pscale, w_out, e, tm=lc, blk=lc)
            x_lat = x_lat_new
        else:
            o = layer // 2
            h_lat = _prenorm(x_lat, pk_lat[1], l, tm=1024)
            h_ctx = _prenorm(x_ctx, pk_ctx[1], b * lc, tm=1024)
            lbr, lbi, bbr, bbi = _s5_discretise(s5_lambda_re[o], s5_lambda_im[o], s5_log_step[o], s5_b_re[o], s5_b_im[o])
            n_groups = lbr.shape[1]
            gsub = SUBLANES
            nsub = min(4, n_groups // gsub)
            ys = []
            for dr, rev in ((0, False), (1, True)):
                a_re, a_im, wb_s5, wc_s5 = _s5_layout(lbr[dr], lbi[dr], bbr[dr], bbi[dr], s5_c_re[o, dr], s5_c_im[o, dr],
                                                      b, gsub, nsub)
                ys.append(_s5_scan(h_ctx.reshape(b, lc, d), h_lat.reshape(b, l, d), a_re, a_im, wb_s5, wc_s5,
                                   reverse=rev, tc=64))
            x_lat = _glu(x_lat, ys[0].reshape(b * l, d), ys[1].reshape(b * l, d), pk_lat[1], l, s5_d[:, None, :], wa, wbb, o,
                         tm=512, rc=256)
            assert not ctx_out, "context output of an S5 layer is not needed by this block"

        x_lat = ffn_lat(x_lat, pk_lat[2], l, wg, wu, wd, layer, 1)
        if ctx_out:
            x_ctx = ffn_ctx(x_ctx, pk_ctx[2], b * lc, wg, wu, wd, layer, 1)

    return x_lat.reshape(b, l, d)
```

```python
import functools

import jax
import jax.numpy as jnp
import numpy as np
from jax import lax
from jax.experimental import pallas as pl
from jax.experimental.pallas import tpu as pltpu

EPS = 1e-6
GRID_W = 64
POOL_WINDOWS = (2, 4, 8, 16)
N_MOD = 9

F32 = jnp.float32
BF16 = jnp.bfloat16

LANES = 128
SUBLANES = 8
VMEM_LIMIT_BYTES = 56 * 1024 * 1024
ROW_SHIFT, ROW_SCALE, ROW_GATE, ROW_GPRE, ROW_GPOST = 0, 1, 2, 3, 4


def _cparams(*sem, flags=None):
    return pltpu.CompilerParams(dimension_semantics=sem, vmem_limit_bytes=VMEM_LIMIT_BYTES, flags=flags)


def _dot(a, b):
    return jnp.dot(a, b, preferred_element_type=F32)


def _pre_rows(x, prm):
    r = lax.rsqrt(jnp.mean(x * x, axis=-1, keepdims=True) + EPS)
    return (x * r * prm[ROW_GPRE:ROW_GPRE + 1]) * (1.0 + prm[ROW_SCALE:ROW_SCALE + 1]) + prm[ROW_SHIFT:ROW_SHIFT + 1]


def _post_rows(x, y, prm, weight):
    r = lax.rsqrt(jnp.mean(y * y, axis=-1, keepdims=True) + EPS)
    return x + (weight * prm[ROW_GATE:ROW_GATE + 1]) * (y * r * prm[ROW_GPOST:ROW_GPOST + 1])


def _rms_stats(src_ref, r_ref, rc):
    tm, d = src_ref.shape

    def body(rows):
        v = src_ref[rows, :]
        ss = jnp.sum(v * v, axis=-1, keepdims=True)
        r_ref[rows, :] = jnp.broadcast_to(lax.rsqrt(ss / d + EPS), (rows.size, LANES))
    _row_chunks(tm, rc, body, unroll=2)


def _pre_apply(x_ref, r_ref, prm, h_ref, rc, zero_ref=None):
    tm, d = x_ref.shape

    def body(rows):
        r = r_ref[rows, :]
        for k in range(d // LANES):
            cols = slice(k * LANES, (k + 1) * LANES)
            h = (x_ref[rows, cols] * r * prm[ROW_GPRE:ROW_GPRE + 1, cols]) * (1.0 + prm[ROW_SCALE:ROW_SCALE + 1, cols])
            h_ref[rows, cols] = (h + prm[ROW_SHIFT:ROW_SHIFT + 1, cols]).astype(h_ref.dtype)
            if zero_ref is not None:
                zero_ref[rows, cols] = jnp.zeros((rows.size, LANES), zero_ref.dtype)
    _row_chunks(tm, rc, body)


def _post_apply(x_ref, y_ref, r_ref, prm, o_ref, weight, rc):
    tm, d = x_ref.shape

    def body(rows):
        r = r_ref[rows, :]
        for k in range(d // LANES):
            cols = slice(k * LANES, (k + 1) * LANES)
            yn = y_ref[rows, cols] * r * prm[ROW_GPOST:ROW_GPOST + 1, cols]
            o_ref[rows, cols] = x_ref[rows, cols] + (weight * prm[ROW_GATE:ROW_GATE + 1, cols]) * yn
    _row_chunks(tm, rc, body)


def _pick_tile(total, target, quantum=LANES):
    if total <= target:
        return total
    best = quantum
    for t in range(quantum, target + 1, quantum):
        if total % t == 0:
            best = t
    assert total % best == 0, (total, target, quantum)
    return best


def _row_chunks(total, chunk, body, unroll=1):
    chunk = min(chunk, total)

    def step(k, carry):
        body(pl.ds(pl.multiple_of(k * chunk, chunk), chunk))
        return carry

    lax.fori_loop(0, total // chunk, step, 0, unroll=unroll)


def _mod_kernel(c_ref, w_ref, b_ref, o_ref):
    sc = jax.nn.silu(c_ref[...]).astype(BF16)
    o_ref[0] = _dot(sc, w_ref[0].astype(BF16)) + b_ref[0]


def _modulation(c_all, w_mod, b_mod):
    depth, d, nd = w_mod.shape
    rows = c_all.shape[0]
    tn = _pick_tile(nd, 1024)
    return pl.pallas_call(
        _mod_kernel,
        grid=(depth, nd // tn),
        in_specs=[
            pl.BlockSpec((rows, d), lambda l, j: (0, 0)),
            pl.BlockSpec((1, d, tn), lambda l, j: (l, 0, j)),
            pl.BlockSpec((1, 1, tn), lambda l, j: (l, 0, j)),
        ],
        out_specs=pl.BlockSpec((1, rows, tn), lambda l, j: (l, 0, j)),
        out_shape=jax.ShapeDtypeStruct((depth, rows, nd), F32),
        compiler_params=_cparams("arbitrary", "arbitrary"),
        name="modulation",
    )(c_all, w_mod, b_mod.reshape(depth, 1, nd))


def _ffn_kernel(x_ref, prm_ref, wg_ref, wu_ref, wd_ref, o_ref, h_ref, r_ref, *, nj, rc):
    j = pl.program_id(1)
    prm = prm_ref[0]

    @pl.when(j == 0)
    def _():
        _rms_stats(x_ref, r_ref, rc)
        _pre_apply(x_ref, r_ref, prm, h_ref, rc, zero_ref=o_ref)

    h = h_ref[...]
    ga = _dot(h, wg_ref[...])
    up = _dot(h, wu_ref[...])
    t = (jax.nn.silu(ga) * up).astype(BF16)
    o_ref[...] += _dot(t, wd_ref[...])

    @pl.when(j == nj - 1)
    def _():
        _rms_stats(o_ref, r_ref, rc)
        _post_apply(x_ref, o_ref, r_ref, prm, o_ref, 0.5, rc)


def _ffn(x2, prm, rows_per_sample, wg, wu, wd, l, k, *, tm, tf):
    n, d = x2.shape
    f = wg.shape[-1]
    tm = min(tm, rows_per_sample)
    tf = _pick_tile(f, tf)
    nj = f // tf
    tiles_per_sample = max(rows_per_sample // tm, 1)
    n_samples = prm.shape[0]
    kern = functools.partial(_ffn_kernel, nj=nj, rc=128)
    return pl.pallas_call(
        kern,
        grid=(n // tm, nj),
        in_specs=[
            pl.BlockSpec((tm, d), lambda i, j: (i, 0)),
            pl.BlockSpec((1, 8, d), lambda i, j: (jnp.minimum(i // tiles_per_sample, n_samples - 1), 0, 0)),
            pl.BlockSpec((None, None, d, tf), lambda i, j: (l, k, 0, j)),
            pl.BlockSpec((None, None, d, tf), lambda i, j: (l, k, 0, j)),
            pl.BlockSpec((None, None, tf, d), lambda i, j: (l, k, j, 0)),
        ],
        out_specs=pl.BlockSpec((tm, d), lambda i, j: (i, 0)),
        out_shape=jax.ShapeDtypeStruct((n, d), F32),
        scratch_shapes=[pltpu.VMEM((tm, d), BF16), pltpu.VMEM((tm, LANES), F32)],
        compiler_params=_cparams("arbitrary", "arbitrary"),
        name="ffn",
    )(x2, prm, wg, wu, wd)


def _preproj_kernel(x_ref, prm_ref, w_ref, o_ref, h_ref, r_ref, *, rc):
    j = pl.program_id(1)

    @pl.when(j == 0)
    def _():
        _rms_stats(x_ref, r_ref, rc)
        _pre_apply(x_ref, r_ref, prm_ref[0], h_ref, rc)

    o_ref[...] = _dot(h_ref[...], w_ref[...])


def _preproj(x2, prm, rows_per_sample, w, e, *, tm, tn):
    n, d = x2.shape
    nc = w.shape[-1]
    tm = min(tm, rows_per_sample)
    tn = _pick_tile(nc, tn)
    tiles_per_sample = max(rows_per_sample // tm, 1)
    n_samples = prm.shape[0]
    return pl.pallas_call(
        functools.partial(_preproj_kernel, rc=128),
        grid=(n // tm, nc // tn),
        in_specs=[
            pl.BlockSpec((tm, d), lambda i, j: (i, 0)),
            pl.BlockSpec((1, 8, d), lambda i, j: (jnp.minimum(i // tiles_per_sample, n_samples - 1), 0, 0)),
            pl.BlockSpec((None, d, tn), lambda i, j: (e, 0, j)),
        ],
        out_specs=pl.BlockSpec((tm, tn), lambda i, j: (i, j)),
        out_shape=jax.ShapeDtypeStruct((n, nc), F32),
        scratch_shapes=[pltpu.VMEM((tm, d), BF16), pltpu.VMEM((tm, LANES), F32)],
        compiler_params=_cparams("arbitrary", "arbitrary"),
        name="mix_in_proj",
    )(x2, prm, w)


def _evenmix_kernel(p_ref, pprev_ref, pnext_ref, x_ref, prm_ref, cw_ref, pw_ref, ps_ref, wo_ref, o_ref,
                    cv_ref, pa_ref, pb_ref, ycat_ref, *, blk, tiles_per_sample, mix_a, pool_ch):
    i = pl.program_id(0)
    tm = x_ref.shape[0]
    prm = prm_ref[0]
    nblk = tm // blk
    pitch = blk + 2 * SUBLANES
    npad = nblk * pitch + SUBLANES
    t_in_sample = i % tiles_per_sample
    has_prev = (t_in_sample != 0).astype(F32)
    has_next = (t_in_sample != tiles_per_sample - 1).astype(F32)

    c0, c1, c2, c3 = 0, mix_a, 2 * mix_a, 3 * mix_a
    cv = p_ref[:, c1:c2] * p_ref[:, c2:c3]
    cv_ref[SUBLANES:SUBLANES + tm, :] = cv
    cv_ref[0:SUBLANES, :] = pprev_ref[:, c1:c2] * pprev_ref[:, c2:c3] * has_prev
    cv_ref[SUBLANES + tm:2 * SUBLANES + tm, :] = pnext_ref[:, c1:c2] * pnext_ref[:, c2:c3] * has_next
    conv = (cw_ref[0:1, :] * cv_ref[SUBLANES - 1:SUBLANES - 1 + tm, :] + cw_ref[1:2, :] * cv
            + cw_ref[2:3, :] * cv_ref[SUBLANES + 1:SUBLANES + 1 + tm, :])
    ycat_ref[:, 0:mix_a] = (p_ref[:, c0:c1] * conv).astype(BF16)

    pos = lax.broadcasted_iota(jnp.int32, (tm, 1), 0) % blk
    inner = npad - 2 * SUBLANES
    for gi, w in enumerate(POOL_WINDOWS):
        lo, hi = c3 + gi * pool_ch, c3 + (gi + 1) * pool_ch
        pa_ref[...] = jnp.zeros_like(pa_ref)
        for b in range(nblk):
            pa_ref[b * pitch + SUBLANES:b * pitch + SUBLANES + blk, :] = p_ref[b * blk:(b + 1) * blk, lo:hi]
        src, dst = pa_ref, pb_ref
        dst[...] = jnp.zeros_like(dst)
        dst[SUBLANES:SUBLANES + inner, :] = src[SUBLANES - 1:SUBLANES - 1 + inner, :] + src[SUBLANES:SUBLANES + inner, :]
        src, dst = dst, src
        half = 1
        while 4 * half <= w:
            dst[...] = jnp.zeros_like(dst)
            dst[SUBLANES:SUBLANES + inner, :] = (src[SUBLANES - half:SUBLANES - half + inner, :]
                                                 + src[SUBLANES + half:SUBLANES + half + inner, :])
            src, dst = dst, src
            half *= 2
        cnt = (jnp.minimum(pos + w // 2, blk) - jnp.maximum(pos - w // 2, 0)).astype(F32)
        pieces = [src[b * pitch + SUBLANES:b * pitch + SUBLANES + blk, :] for b in range(nblk)]
        total = pieces[0] if nblk == 1 else jnp.concatenate(pieces, axis=0)
        pooled = total / cnt - p_ref[:, lo:hi]
        yb = _dot(pooled.astype(BF16), pw_ref[gi]) * ps_ref[:, gi * pool_ch:(gi + 1) * pool_ch]
        ycat_ref[:, mix_a + gi * pool_ch:mix_a + (gi + 1) * pool_ch] = yb.astype(BF16)

    y = _dot(ycat_ref[...], wo_ref[...])
    o_ref[...] = _post_rows(x_ref[...], y, prm, 1.0)


def _evenmix(p2, x2, prm, rows_per_sample, conv_w8, pool_w, pool_scale, w_out, e, *, tm, blk):
    n, d = x2.shape
    nc = p2.shape[1]
    mix_a = conv_w8.shape[-1]
    pool_ch = pool_w.shape[-1]
    mix_b = pool_ch * len(POOL_WINDOWS)
    tm = min(tm, rows_per_sample)
    tiles_per_sample = rows_per_sample // tm
    n_samples = prm.shape[0]
    nblk = tm // blk
    npad = nblk * (blk + 2 * SUBLANES) + SUBLANES
    tb = tm // SUBLANES
    nb8 = n // SUBLANES
    kern = functools.partial(_evenmix_kernel, blk=blk, tiles_per_sample=tiles_per_sample, mix_a=mix_a, pool_ch=pool_ch)
    return pl.pallas_call(
        kern,
        grid=(n // tm,),
        in_specs=[
            pl.BlockSpec((tm, nc), lambda i: (i, 0)),
            pl.BlockSpec((SUBLANES, nc), lambda i: (jnp.maximum(i * tb - 1, 0), 0)),
            pl.BlockSpec((SUBLANES, nc), lambda i: (jnp.minimum((i + 1) * tb, nb8 - 1), 0)),
            pl.BlockSpec((tm, d), lambda i: (i, 0)),
            pl.BlockSpec((1, 8, d), lambda i: (jnp.minimum(i // tiles_per_sample, n_samples - 1), 0, 0)),
            pl.BlockSpec((None, 8, mix_a), lambda i: (e, 0, 0)),
            pl.BlockSpec((None, len(POOL_WINDOWS), pool_ch, pool_ch), lambda i: (e, 0, 0, 0)),
            pl.BlockSpec((None, 1, mix_b), lambda i: (e, 0, 0)),
            pl.BlockSpec((None, mix_a + mix_b, d), lambda i: (e, 0, 0)),
        ],
        out_specs=pl.BlockSpec((tm, d), lambda i: (i, 0)),
        out_shape=jax.ShapeDtypeStruct((n, d), F32),
        scratch_shapes=[
            pltpu.VMEM((tm + 2 * SUBLANES, mix_a), F32),
            pltpu.VMEM((npad, pool_ch), F32),
            pltpu.VMEM((npad, pool_ch), F32),
            pltpu.VMEM((tm, mix_a + mix_b), BF16),
        ],
        compiler_params=_cparams("arbitrary"),
        name="even_mixer",
    )(p2, p2, p2, x2, prm, conv_w8, pool_w, pool_scale, w_out)


def _prenorm_kernel(x_ref, prm_ref, o_ref, r_ref, *, rc):
    _rms_stats(x_ref, r_ref, rc)
    _pre_apply(x_ref, r_ref, prm_ref[0], o_ref, rc)


def _prenorm(x2, prm, rows_per_sample, *, tm):
    n, d = x2.shape
    tm = min(tm, rows_per_sample)
    tiles_per_sample = rows_per_sample // tm
    n_samples = prm.shape[0]
    return pl.pallas_call(
        functools.partial(_prenorm_kernel, rc=128),
        grid=(n // tm,),
        in_specs=[
            pl.BlockSpec((tm, d), lambda i: (i, 0)),
            pl.BlockSpec((1, 8, d), lambda i: (jnp.minimum(i // tiles_per_sample, n_samples - 1), 0, 0)),
        ],
        out_specs=pl.BlockSpec((tm, d), lambda i: (i, 0)),
        out_shape=jax.ShapeDtypeStruct((n, d), F32),
        scratch_shapes=[pltpu.VMEM((tm, LANES), F32)],
        compiler_params=_cparams("arbitrary"),
        name="s5_prenorm",
    )(x2, prm)


def _s5_drive_pieces(j, hc_ref, hl_ref, use_ctx, wb_ref, slab_ref, hbf_ref, buf_ref, *, tc, ch_sub, st_sub, npiece):
    nb = hl_ref.shape[0]
    pitch = tc + SUBLANES
    ch = slice(j * ch_sub, (j + 1) * ch_sub)

    def relayout():
        for k in range(j * ch_sub // LANES, (j + 1) * ch_sub // LANES):
            cols = slice(k * LANES, (k + 1) * LANES)
            for b in range(nb):
                slab_ref[k, b * pitch:b * pitch + tc, :] = jnp.where(use_ctx, hc_ref[b, :, cols], hl_ref[b, :, cols])
            for t in range(0, tc, 2):
                pair = [slab_ref.at[k][pl.ds(t + i, nb, stride=pitch), :] for i in range(2)]
                hbf_ref[t * nb:(t + 2) * nb, cols] = jnp.concatenate(pair, axis=0).astype(BF16)

    def matmul(p):
        w = st_sub // npiece
        cols = slice(j * st_sub + p * w, j * st_sub + (p + 1) * w)
        buf_ref[:, cols] = _dot(hbf_ref[:, ch], wb_ref[0, j, :, p * w:(p + 1) * w])

    return [relayout] + [functools.partial(matmul, p) for p in range(npiece)]


def _s5_readout_pieces(j, wc_ref, buf_ref, slab_ref, o_ref, *, tc, ch_sub, st_sub, npiece):
    nb = o_ref.shape[0]
    ch = slice(j * ch_sub, (j + 1) * ch_sub)

    def matmul(p):
        m = tc * nb // npiece
        rows = slice(p * m, (p + 1) * m)
        y = _dot(buf_ref[rows, j * st_sub:(j + 1) * st_sub], wc_ref[0, j])
        for i, k in enumerate(range(j * ch_sub // LANES, (j + 1) * ch_sub // LANES)):
            slab_ref[k, rows, :] = y[:, i * LANES:(i + 1) * LANES]

    def relayout():
        for k in range(j * ch_sub // LANES, (j + 1) * ch_sub // LANES):
            cols = slice(k * LANES, (k + 1) * LANES)
            for b in range(nb):
                o_ref[b, :, cols] = slab_ref.at[k][pl.ds(b, tc, stride=nb), :]

    return [functools.partial(matmul, p) for p in range(npiece)] + [relayout]


def _s5_scan_steps(j, sr, si, ar, ai, steps, buf_ref, sbf_ref, *, reverse, tc, st_sub, nb):
    half = st_sub // 2
    lo = j * st_sub
    assert len(steps) % 2 == 0 and steps[0] % 2 == 0
    for s0 in steps[::2]:
        pair = []
        for s in (s0, s0 + 1):
            t = (tc - 1 - s) if reverse else s
            rows = slice(t * nb, (t + 1) * nb)
            nr = ar * sr - ai * si + buf_ref[rows, lo:lo + half]
            ni = ar * si + ai * sr + buf_ref[rows, lo + half:lo + st_sub]
            sr, si = nr, ni
            pair.append((t, nr, ni))
        pair.sort(key=lambda e: e[0])
        rows2 = slice(pair[0][0] * nb, (pair[0][0] + 2) * nb)
        sbf_ref[rows2, lo:lo + half] = jnp.concatenate([pair[0][1], pair[1][1]], axis=0).astype(BF16)
        sbf_ref[rows2, lo + half:lo + st_sub] = jnp.concatenate([pair[0][2], pair[1][2]], axis=0).astype(BF16)
    return sr, si


def _s5_kernel(hc_ref, hl_ref, are_ref, aim_ref, wb_ref, wc_ref, o_ref,
               slab_in_ref, slab_out_ref, hbf_ref, buf0_ref, buf1_ref, sbf0_ref, sbf1_ref, st_ref,
               *, reverse, n_ctx_chunks, tc, nsub, ch_sub, st_sub):
    s = pl.program_id(1)
    nb = st_ref.shape[0]
    half = st_sub // 2
    bufs = (buf0_ref, buf1_ref)
    sbfs = (sbf0_ref, sbf1_ref)

    @pl.when(s == 0)
    def _():
        for r in bufs + sbfs:
            r[...] = jnp.zeros_like(r)
        st_ref[...] = jnp.zeros_like(st_ref)

    use_ctx = s < n_ctx_chunks
    for r in range(2):
        @pl.when(s % 2 == r)
        def _(r=r):
            drive_buf, scan_buf, state_buf, read_buf = bufs[r], bufs[1 - r], sbfs[1 - r], sbfs[r]
            scan_kw = dict(reverse=reverse, tc=tc, st_sub=st_sub, nb=nb)
            for j in range(nsub):
                re_cols = slice(2 * j * half, (2 * j + 1) * half)
                im_cols = slice((2 * j + 1) * half, (2 * j + 2) * half)
                ar = are_ref[0, :, j * half:(j + 1) * half]
                ai = aim_ref[0, :, j * half:(j + 1) * half]
                sr, si = st_ref[:, re_cols], st_ref[:, im_cols]
                dp = _s5_drive_pieces(j, hc_ref, hl_ref, use_ctx, wb_ref, slab_in_ref, hbf_ref, drive_buf,
                                      tc=tc, ch_sub=ch_sub, st_sub=st_sub, npiece=4)
                rp = _s5_readout_pieces(j, wc_ref, read_buf, slab_out_ref, o_ref,
                                        tc=tc, ch_sub=ch_sub, st_sub=st_sub, npiece=2)
                pieces = [dp[0], rp[0], dp[1], dp[2], rp[1], dp[3], dp[4], rp[2]]
                per = tc // len(pieces)
                for g, piece in enumerate(pieces):
                    last = tc if g == len(pieces) - 1 else (g + 1) * per
                    sr, si = _s5_scan_steps(j, sr, si, ar, ai, range(g * per, last), scan_buf, state_buf, **scan_kw)
                    piece()
                st_ref[:, re_cols] = sr
                st_ref[:, im_cols] = si


def _s5_scan(h_ctx3, h_lat3, a_re, a_im, wb, wc, *, reverse, tc):
    nb, lc, d = h_ctx3.shape
    l = h_lat3.shape[1]
    nq, nsub, ch_sub, st_sub = wb.shape
    dq = nsub * ch_sub
    tc = min(tc, lc)
    ncc, nlc = lc // tc, l // tc
    pitch = tc + SUBLANES

    n_chunks = ncc + nlc

    def ctx_idx(k):
        cc = jnp.clip(k, 0, ncc - 1)
        return (ncc - 1 - cc) if reverse else cc

    def lat_idx(k):
        lc_ = jnp.clip(k - ncc, 0, nlc - 1)
        return (nlc - 1 - lc_) if reverse else lc_

    kern = functools.partial(_s5_kernel, reverse=reverse, n_ctx_chunks=ncc, tc=tc, nsub=nsub, ch_sub=ch_sub, st_sub=st_sub)
    drive_buf = pltpu.VMEM((tc * nb, nsub * st_sub), F32)
    state_buf = pltpu.VMEM((tc * nb, nsub * st_sub), BF16)
    return pl.pallas_call(
        kern,
        grid=(nq, n_chunks + 2),
        in_specs=[
            pl.BlockSpec((nb, tc, dq), lambda q, s: (0, ctx_idx(jnp.minimum(s, n_chunks - 1)), q)),
            pl.BlockSpec((nb, tc, dq), lambda q, s: (0, lat_idx(jnp.minimum(s, n_chunks - 1)), q)),
            pl.BlockSpec((1, nb, nsub * st_sub // 2), lambda q, s: (q, 0, 0)),
            pl.BlockSpec((1, nb, nsub * st_sub // 2), lambda q, s: (q, 0, 0)),
            pl.BlockSpec((1, nsub, ch_sub, st_sub), lambda q, s: (q, 0, 0, 0)),
            pl.BlockSpec((1, nsub, st_sub, ch_sub), lambda q, s: (q, 0, 0, 0)),
        ],
        out_specs=pl.BlockSpec((nb, tc, dq), lambda q, s: (0, lat_idx(s - 2), q)),
        out_shape=jax.ShapeDtypeStruct((nb, l, d), F32),
        scratch_shapes=[
            pltpu.VMEM((dq // LANES, nb * pitch, LANES), F32),
            pltpu.VMEM((dq // LANES, nb * tc, LANES), F32),
            pltpu.VMEM((tc * nb, dq), BF16),
            drive_buf, drive_buf,
            state_buf, state_buf,
            pltpu.VMEM((nb, nsub * st_sub), F32),
        ],
        compiler_params=_cparams("arbitrary", "arbitrary"),
        name="s5_scan_rev" if reverse else "s5_scan_fwd",
    )(h_ctx3, h_lat3, a_re, a_im, wb, wc)


def _glu_kernel(x_ref, yf_ref, yr_ref, prm_ref, dv_ref, wa_ref, wb_ref, o_ref, *, rc):
    tm = x_ref.shape[0]
    prm = prm_ref[0]
    for c in range(tm // rc):
        rows = slice(c * rc, (c + 1) * rc)
        x = x_ref[rows, :]
        h = _pre_rows(x, prm)
        y = h * dv_ref[...] + yf_ref[rows, :] + yr_ref[rows, :]
        gy = jax.nn.gelu(y).astype(BF16)
        z = _dot(gy, wa_ref[...]) * jax.nn.sigmoid(_dot(gy, wb_ref[...]))
        o_ref[rows, :] = _post_rows(x, z, prm, 1.0)


def _glu(x2, yf2, yr2, prm, rows_per_sample, dv, wa, wb, o, *, tm, rc):
    n, d = x2.shape
    tm = min(tm, rows_per_sample)
    rc = min(rc, tm)
    tiles_per_sample = rows_per_sample // tm
    n_samples = prm.shape[0]
    return pl.pallas_call(
        functools.partial(_glu_kernel, rc=rc),
        grid=(n // tm,),
        in_specs=[
            pl.BlockSpec((tm, d), lambda i: (i, 0)),
            pl.BlockSpec((tm, d), lambda i: (i, 0)),
            pl.BlockSpec((tm, d), lambda i: (i, 0)),
            pl.BlockSpec((1, 8, d), lambda i: (jnp.minimum(i // tiles_per_sample, n_samples - 1), 0, 0)),
            pl.BlockSpec((None, 1, d), lambda i: (o, 0, 0)),
            pl.BlockSpec((None, d, d), lambda i: (o, 0, 0), pipeline_mode=pl.Buffered(1)),
            pl.BlockSpec((None, d, d), lambda i: (o, 0, 0), pipeline_mode=pl.Buffered(1)),
        ],
        out_specs=pl.BlockSpec((tm, d), lambda i: (i, 0)),
        out_shape=jax.ShapeDtypeStruct((n, d), F32),
        compiler_params=_cparams("arbitrary"),
        name="s5_glu",
    )(x2, yf2, yr2, prm, dv, wa, wb)


def _param_pack(m, g, i):
    s, _, d = m.shape
    rows = [m[:, 3 * i], m[:, 3 * i + 1], m[:, 3 * i + 2],
            jnp.broadcast_to(g[2 * i], (s, d)), jnp.broadcast_to(g[2 * i + 1], (s, d))]
    rows += [jnp.zeros((s, d), F32)] * 3
    return jnp.stack(rows, axis=1)


def _s5_discretise(lam_re, lam_im, log_step, b_re, b_im):
    dt = jnp.exp(log_step)[..., None]
    mag = jnp.exp(lam_re * dt)
    lbar_re = mag * jnp.cos(lam_im * dt)
    lbar_im = mag * jnp.sin(lam_im * dt)
    den = lam_re * lam_re + lam_im * lam_im
    nr = lbar_re - 1.0
    fr = (nr * lam_re + lbar_im * lam_im) / den
    fi = (lbar_im * lam_re - nr * lam_im) / den
    bbar_re = fr[..., None] * b_re - fi[..., None] * b_im
    bbar_im = fr[..., None] * b_im + fi[..., None] * b_re
    return lbar_re, lbar_im, bbar_re, bbar_im


def _s5_layout(lbar_re, lbar_im, bbar_re, bbar_im, c_re, c_im, nb, gsub, nsub):
    g, p, h = bbar_re.shape
    nblk = g // gsub
    eye = jnp.eye(gsub, dtype=F32)

    def drive(bb):
        bb = bb.reshape(nblk, gsub, p, h)
        return jnp.einsum('ngph,gk->nghkp', bb, eye).reshape(nblk, gsub * h, gsub * p)

    def read(cc):
        cc = cc.reshape(nblk, gsub, h, p)
        return jnp.einsum('nghp,gk->ngpkh', cc, eye).reshape(nblk, gsub * p, gsub * h)

    wb = jnp.concatenate([drive(bbar_re), drive(bbar_im)], axis=2)
    wc = jnp.concatenate([read(c_re), -read(c_im)], axis=1)
    nq = nblk // nsub
    wb = wb.reshape(nq, nsub, gsub * h, 2 * gsub * p).astype(BF16)
    wc = wc.reshape(nq, nsub, 2 * gsub * p, gsub * h).astype(BF16)
    a_re = jnp.broadcast_to(lbar_re.reshape(nq, 1, nsub * gsub * p), (nq, nb, nsub * gsub * p))
    a_im = jnp.broadcast_to(lbar_im.reshape(nq, 1, nsub * gsub * p), (nq, nb, nsub * gsub * p))
    return a_re, a_im, wb, wc


def kernel(x, c, ctx, c_ctx, w_mod, b_mod, norm_g, ffn_w_gate, ffn_w_up, ffn_w_down, mix_in, conv_w, pool_w, pool_scale, mix_out, s5_lambda_re, s5_lambda_im, s5_log_step, s5_b_re, s5_b_im, s5_c_re, s5_c_im, s5_d, glu_w_a, glu_w_b):
    b, l, d = x.shape
    lc = ctx.shape[1]
    depth = w_mod.shape[0]
    assert b == SUBLANES, "the S5 recurrence keeps the batch on the 8 sublanes"
    assert l % GRID_W == 0 and d % LANES == 0

    wg, wu, wd = ffn_w_gate.astype(BF16), ffn_w_up.astype(BF16), ffn_w_down.astype(BF16)
    w_in, w_out = mix_in.astype(BF16), mix_out.astype(BF16)
    pw = pool_w.astype(BF16)
    wa, wbb = glu_w_a.astype(BF16), glu_w_b.astype(BF16)
    conv_w8 = jnp.pad(conv_w, ((0, 0), (0, 8 - conv_w.shape[1]), (0, 0)))
    pscale = pool_scale[:, None, :]

    c_all = jnp.concatenate([c, c_ctx[None], jnp.zeros((2 * SUBLANES - b - 1, d), F32)], axis=0)
    m_all = _modulation(c_all, w_mod, b_mod)[:, :b + 1].reshape(depth, b + 1, N_MOD, d)

    x_lat = x.reshape(b * l, d)
    x_ctx = ctx.reshape(b * lc, d)
    ffn_lat = functools.partial(_ffn, tm=1024, tf=512)
    ffn_ctx = functools.partial(_ffn, tm=1024, tf=512)

    for layer in range(depth):
        last = layer == depth - 1
        even = layer % 2 == 0
        run_ctx_in = not (last and even)
        ctx_out = not last
        g = norm_g[layer]
        m_lat, m_ctx = m_all[layer, :b], m_all[layer, b:]
        pk_lat = [_param_pack(m_lat, g, i) for i in range(3)]
        pk_ctx = [_param_pack(m_ctx, g, i) for i in range(3)]

        x_lat = ffn_lat(x_lat, pk_lat[0], l, wg, wu, wd, layer, 0)
        if run_ctx_in:
            x_ctx = ffn_ctx(x_ctx, pk_ctx[0], b * lc, wg, wu, wd, layer, 0)

        if even:
            e = layer // 2
            p_lat = _preproj(x_lat, pk_lat[1], l, w_in, e, tm=1024, tn=1024)
            x_lat_new = _evenmix(p_lat, x_lat, pk_lat[1], l, conv_w8, pw, pscale, w_out, e, tm=256, blk=GRID_W)
            if ctx_out:
                p_ctx = _preproj(x_ctx, pk_ctx[1], b * lc, w_in, e, tm=1024, tn=1024)
                x_ctx = _evenmix(p_ctx, x_ctx, pk_ctx[1], lc, conv_w8, pw, pscale, w_out, e, tm=lc, blk=lc)
            x_lat = x_lat_new
        else:
            o = layer // 2
            h_lat = _prenorm(x_lat, pk_lat[1], l, tm=1024)
            h_ctx = _prenorm(x_ctx, pk_ctx[1], b * lc, tm=1024)
            lbr, lbi, bbr, bbi = _s5_discretise(s5_lambda_re[o], s5_lambda_im[o], s5_log_step[o], s5_b_re[o], s5_b_im[o])
            n_groups = lbr.shape[1]
            gsub = SUBLANES
            nsub = min(4, n_groups // gsub)
            ys = []
            for dr, rev in ((0, False), (1, True)):
                a_re, a_im, wb_s5, wc_s5 = _s5_layout(lbr[dr], lbi[dr], bbr[dr], bbi[dr], s5_c_re[o, dr], s5_c_im[o, dr],
                                                      b, gsub, nsub)
                ys.append(_s5_scan(h_ctx.reshape(b, lc, d), h_lat.reshape(b, l, d), a_re, a_im, wb_s5, wc_s5,
                                   reverse=rev, tc=64))
            x_lat = _glu(x_lat, ys[0].reshape(b * l, d), ys[1].reshape(b * l, d), pk_lat[1], l, s5_d[:, None, :], wa, wbb, o,
                         tm=512, rc=256)
            assert not ctx_out, "context output of an S5 layer is not needed by this block"

        x_lat = ffn_lat(x_lat, pk_lat[2], l, wg, wu, wd, layer, 1)
        if ctx_out:
            x_ctx = ffn_ctx(x_ctx, pk_ctx[2], b * lc, wg, wu, wd, layer, 1)

    return x_lat.reshape(b, l, d)
```

```python
import functools

import jax
import jax.numpy as jnp
import numpy as np
from jax import lax
from jax.experimental import pallas as pl
from jax.experimental.pallas import tpu as pltpu

EPS = 1e-6
GRID_W = 64
POOL_WINDOWS = (2, 4, 8, 16)
N_MOD = 9

F32 = jnp.float32
BF16 = jnp.bfloat16

LANES = 128
SUBLANES = 8
VMEM_LIMIT_BYTES = 56 * 1024 * 1024
FFN_ROWS, FFN_FF_COLS = 1024, 512
FFN_EDGE_CHUNK = 256
PROJ_ROWS, PROJ_COLS = 1024, 1024
MIX_ROWS = 512
NORM_ROWS = 1024
GLU_ROWS, GLU_CHUNK = 512, 256
S5_TIME_CHUNK = 64
S5_GROUPS_PER_SUBBLOCK = SUBLANES
S5_SUBBLOCKS_PER_STEP = 4
ROW_SHIFT, ROW_SCALE, ROW_GATE, ROW_GPRE, ROW_GPOST = 0, 1, 2, 3, 4


def _cparams(*sem, flags=None):
    return pltpu.CompilerParams(dimension_semantics=sem, vmem_limit_bytes=VMEM_LIMIT_BYTES, flags=flags)


def _dot(a, b):
    return jnp.dot(a, b, preferred_element_type=F32)


def _pre_rows(x, prm):
    r = lax.rsqrt(jnp.mean(x * x, axis=-1, keepdims=True) + EPS)
    return (x * r * prm[ROW_GPRE:ROW_GPRE + 1]) * (1.0 + prm[ROW_SCALE:ROW_SCALE + 1]) + prm[ROW_SHIFT:ROW_SHIFT + 1]


def _post_rows(x, y, prm, weight):
    r = lax.rsqrt(jnp.mean(y * y, axis=-1, keepdims=True) + EPS)
    return x + (weight * prm[ROW_GATE:ROW_GATE + 1]) * (y * r * prm[ROW_GPOST:ROW_GPOST + 1])


def _rms_stats(src_ref, r_ref, rc, rows=None):
    tm, d = src_ref.shape

    def body(rows):
        v = src_ref[rows, :]
        ss = jnp.sum(v * v, axis=-1, keepdims=True)
        r_ref[rows, :] = jnp.broadcast_to(lax.rsqrt(ss / d + EPS), (rows.size, LANES))
    _row_chunks(tm, rc, body, unroll=2, rows=rows)


def _pre_apply(x_ref, r_ref, prm, h_ref, rc, zero_ref=None, rows=None):
    tm, d = x_ref.shape

    def body(rows):
        r = r_ref[rows, :]
        for k in range(d // LANES):
            cols = slice(k * LANES, (k + 1) * LANES)
            h = (x_ref[rows, cols] * r * prm[ROW_GPRE:ROW_GPRE + 1, cols]) * (1.0 + prm[ROW_SCALE:ROW_SCALE + 1, cols])
            h_ref[rows, cols] = (h + prm[ROW_SHIFT:ROW_SHIFT + 1, cols]).astype(h_ref.dtype)
            if zero_ref is not None:
                zero_ref[rows, cols] = jnp.zeros((rows.size, LANES), zero_ref.dtype)
    _row_chunks(tm, rc, body, rows=rows)


def _post_apply(x_ref, y_ref, r_ref, prm, o_ref, weight, rc, rows=None):
    tm, d = x_ref.shape

    def body(rows):
        r = r_ref[rows, :]
        for k in range(d // LANES):
            cols = slice(k * LANES, (k + 1) * LANES)
            yn = y_ref[rows, cols] * r * prm[ROW_GPOST:ROW_GPOST + 1, cols]
            o_ref[rows, cols] = x_ref[rows, cols] + (weight * prm[ROW_GATE:ROW_GATE + 1, cols]) * yn
    _row_chunks(tm, rc, body, rows=rows)


def _pick_tile(total, target, quantum=LANES):
    if total <= target:
        return total
    best = quantum
    for t in range(quantum, target + 1, quantum):
        if total % t == 0:
            best = t
    assert total % best == 0, (total, target, quantum)
    return best


def _row_chunks(total, chunk, body, unroll=1, rows=None):
    if rows is not None:
        start, size = rows
        chunk = min(chunk, size)
        for k in range(size // chunk):
            body(pl.ds(start + k * chunk, chunk))
        return
    chunk = min(chunk, total)

    def step(k, carry):
        body(pl.ds(pl.multiple_of(k * chunk, chunk), chunk))
        return carry

    lax.fori_loop(0, total // chunk, step, 0, unroll=unroll)


def _mod_kernel(c_ref, w_ref, b_ref, o_ref):
    sc = jax.nn.silu(c_ref[...]).astype(BF16)
    o_ref[0] = _dot(sc, w_ref[0].astype(BF16)) + b_ref[0]


def _modulation(c_all, w_mod, b_mod):
    depth, d, nd = w_mod.shape
    rows = c_all.shape[0]
    tn = _pick_tile(nd, 1024)
    return pl.pallas_call(
        _mod_kernel,
        grid=(depth, nd // tn),
        in_specs=[
            pl.BlockSpec((rows, d), lambda l, j: (0, 0)),
            pl.BlockSpec((1, d, tn), lambda l, j: (l, 0, j)),
            pl.BlockSpec((1, 1, tn), lambda l, j: (l, 0, j)),
        ],
        out_specs=pl.BlockSpec((1, rows, tn), lambda l, j: (l, 0, j)),
        out_shape=jax.ShapeDtypeStruct((depth, rows, nd), F32),
        compiler_params=_cparams("arbitrary", "arbitrary"),
        name="modulation",
    )(c_all, w_mod, b_mod.reshape(depth, 1, nd))


def _ffn_kernel(x_ref, prm_ref, wg_ref, wu_ref, wd_ref, o_ref, h_ref, r_ref, *, nj, rc, mc):
    j = pl.program_id(1)
    prm = prm_ref[0]
    tm = x_ref.shape[0]

    def swiglu_down(rows):
        h = h_ref[rows, :]
        t = (jax.nn.silu(_dot(h, wg_ref[...])) * _dot(h, wu_ref[...])).astype(BF16)
        return _dot(t, wd_ref[...])

    @pl.when(j == 0)
    def _():
        for c in range(tm // mc):
            _rms_stats(x_ref, r_ref, rc, rows=(c * mc, mc))
            _pre_apply(x_ref, r_ref, prm, h_ref, rc, rows=(c * mc, mc))
            o_ref[c * mc:(c + 1) * mc, :] = swiglu_down(slice(c * mc, (c + 1) * mc))

    @pl.when(jnp.logical_and(j > 0, j < nj - 1))
    def _():
        o_ref[...] += swiglu_down(slice(None))

    @pl.when(j == nj - 1)
    def _():
        for c in range(tm // mc):
            o_ref[c * mc:(c + 1) * mc, :] += swiglu_down(slice(c * mc, (c + 1) * mc))
            _rms_stats(o_ref, r_ref, rc, rows=(c * mc, mc))
            _post_apply(x_ref, o_ref, r_ref, prm, o_ref, 0.5, rc, rows=(c * mc, mc))


def _ffn(x2, prm, rows_per_sample, wg, wu, wd, l, k, *, tm, tf):
    n, d = x2.shape
    f = wg.shape[-1]
    tm = min(tm, rows_per_sample)
    tf = _pick_tile(f, tf)
    nj = f // tf
    tiles_per_sample = max(rows_per_sample // tm, 1)
    n_samples = prm.shape[0]
    assert nj >= 2, "the first and the last d_ff step are distinct code paths"
    kern = functools.partial(_ffn_kernel, nj=nj, rc=128, mc=min(FFN_EDGE_CHUNK, tm))
    return pl.pallas_call(
        kern,
        grid=(n // tm, nj),
        in_specs=[
            pl.BlockSpec((tm, d), lambda i, j: (i, 0)),
            pl.BlockSpec((1, 8, d), lambda i, j: (jnp.minimum(i // tiles_per_sample, n_samples - 1), 0, 0)),
            pl.BlockSpec((None, None, d, tf), lambda i, j: (l, k, 0, j)),
            pl.BlockSpec((None, None, d, tf), lambda i, j: (l, k, 0, j)),
            pl.BlockSpec((None, None, tf, d), lambda i, j: (l, k, j, 0)),
        ],
        out_specs=pl.BlockSpec((tm, d), lambda i, j: (i, 0)),
        out_shape=jax.ShapeDtypeStruct((n, d), F32),
        scratch_shapes=[pltpu.VMEM((tm, d), BF16), pltpu.VMEM((tm, LANES), F32)],
        compiler_params=_cparams("arbitrary", "arbitrary"),
        name="ffn",
    )(x2, prm, wg, wu, wd)


def _preproj_kernel(x_ref, prm_ref, w_ref, o_ref, h_ref, r_ref, *, rc):
    j = pl.program_id(1)

    @pl.when(j == 0)
    def _():
        _rms_stats(x_ref, r_ref, rc)
        _pre_apply(x_ref, r_ref, prm_ref[0], h_ref, rc)

    o_ref[...] = _dot(h_ref[...], w_ref[...])


def _preproj(x2, prm, rows_per_sample, w, e, *, tm, tn):
    n, d = x2.shape
    nc = w.shape[-1]
    tm = min(tm, rows_per_sample)
    tn = _pick_tile(nc, tn)
    tiles_per_sample = max(rows_per_sample // tm, 1)
    n_samples = prm.shape[0]
    return pl.pallas_call(
        functools.partial(_preproj_kernel, rc=128),
        grid=(n // tm, nc // tn),
        in_specs=[
            pl.BlockSpec((tm, d), lambda i, j: (i, 0)),
            pl.BlockSpec((1, 8, d), lambda i, j: (jnp.minimum(i // tiles_per_sample, n_samples - 1), 0, 0)),
            pl.BlockSpec((None, d, tn), lambda i, j: (e, 0, j)),
        ],
        out_specs=pl.BlockSpec((tm, tn), lambda i, j: (i, j)),
        out_shape=jax.ShapeDtypeStruct((n, nc), F32),
        scratch_shapes=[pltpu.VMEM((tm, d), BF16), pltpu.VMEM((tm, LANES), F32)],
        compiler_params=_cparams("arbitrary", "arbitrary"),
        name="mix_in_proj",
    )(x2, prm, w)


def _evenmix_kernel(p_ref, pprev_ref, pnext_ref, x_ref, prm_ref, cw_ref, pw_ref, ps_ref, wo_ref, o_ref,
                    cv_ref, pa_ref, pb_ref, ycat_ref, *, blk, tiles_per_sample, mix_a, pool_ch):
    i = pl.program_id(0)
    tm = x_ref.shape[0]
    prm = prm_ref[0]
    nblk = tm // blk
    pitch = blk + 2 * SUBLANES
    npad = nblk * pitch + SUBLANES
    t_in_sample = i % tiles_per_sample
    has_prev = (t_in_sample != 0).astype(F32)
    has_next = (t_in_sample != tiles_per_sample - 1).astype(F32)

    c0, c1, c2, c3 = 0, mix_a, 2 * mix_a, 3 * mix_a
    cv = p_ref[:, c1:c2] * p_ref[:, c2:c3]
    cv_ref[SUBLANES:SUBLANES + tm, :] = cv
    cv_ref[0:SUBLANES, :] = pprev_ref[:, c1:c2] * pprev_ref[:, c2:c3] * has_prev
    cv_ref[SUBLANES + tm:2 * SUBLANES + tm, :] = pnext_ref[:, c1:c2] * pnext_ref[:, c2:c3] * has_next
    conv = (cw_ref[0:1, :] * cv_ref[SUBLANES - 1:SUBLANES - 1 + tm, :] + cw_ref[1:2, :] * cv
            + cw_ref[2:3, :] * cv_ref[SUBLANES + 1:SUBLANES + 1 + tm, :])
    ycat_ref[:, 0:mix_a] = (p_ref[:, c0:c1] * conv).astype(BF16)

    pos = lax.broadcasted_iota(jnp.int32, (tm, 1), 0) % blk
    inner = npad - 2 * SUBLANES
    for gi, w in enumerate(POOL_WINDOWS):
        lo, hi = c3 + gi * pool_ch, c3 + (gi + 1) * pool_ch
        pa_ref[...] = jnp.zeros_like(pa_ref)
        for b in range(nblk):
            pa_ref[b * pitch + SUBLANES:b * pitch + SUBLANES + blk, :] = p_ref[b * blk:(b + 1) * blk, lo:hi]
        src, dst = pa_ref, pb_ref
        dst[...] = jnp.zeros_like(dst)
        dst[SUBLANES:SUBLANES + inner, :] = src[SUBLANES - 1:SUBLANES - 1 + inner, :] + src[SUBLANES:SUBLANES + inner, :]
        src, dst = dst, src
        half = 1
        while 4 * half <= w:
            dst[...] = jnp.zeros_like(dst)
            dst[SUBLANES:SUBLANES + inner, :] = (src[SUBLANES - half:SUBLANES - half + inner, :]
                                                 + src[SUBLANES + half:SUBLANES + half + inner, :])
            src, dst = dst, src
            half *= 2
        cnt = (jnp.minimum(pos + w // 2, blk) - jnp.maximum(pos - w // 2, 0)).astype(F32)
        pieces = [src[b * pitch + SUBLANES:b * pitch + SUBLANES + blk, :] for b in range(nblk)]
        total = pieces[0] if nblk == 1 else jnp.concatenate(pieces, axis=0)
        pooled = total / cnt - p_ref[:, lo:hi]
        yb = _dot(pooled.astype(BF16), pw_ref[gi]) * ps_ref[:, gi * pool_ch:(gi + 1) * pool_ch]
        ycat_ref[:, mix_a + gi * pool_ch:mix_a + (gi + 1) * pool_ch] = yb.astype(BF16)

    y = _dot(ycat_ref[...], wo_ref[...])
    o_ref[...] = _post_rows(x_ref[...], y, prm, 1.0)


def _evenmix(p2, x2, prm, rows_per_sample, conv_w8, pool_w, pool_scale, w_out, e, *, tm, blk):
    n, d = x2.shape
    nc = p2.shape[1]
    mix_a = conv_w8.shape[-1]
    pool_ch = pool_w.shape[-1]
    mix_b = pool_ch * len(POOL_WINDOWS)
    tm = min(tm, rows_per_sample)
    tiles_per_sample = rows_per_sample // tm
    n_samples = prm.shape[0]
    nblk = tm // blk
    npad = nblk * (blk + 2 * SUBLANES) + SUBLANES
    tb = tm // SUBLANES
    nb8 = n // SUBLANES
    kern = functools.partial(_evenmix_kernel, blk=blk, tiles_per_sample=tiles_per_sample, mix_a=mix_a, pool_ch=pool_ch)
    return pl.pallas_call(
        kern,
        grid=(n // tm,),
        in_specs=[
            pl.BlockSpec((tm, nc), lambda i: (i, 0)),
            pl.BlockSpec((SUBLANES, nc), lambda i: (jnp.maximum(i * tb - 1, 0), 0)),
            pl.BlockSpec((SUBLANES, nc), lambda i: (jnp.minimum((i + 1) * tb, nb8 - 1), 0)),
            pl.BlockSpec((tm, d), lambda i: (i, 0)),
            pl.BlockSpec((1, 8, d), lambda i: (jnp.minimum(i // tiles_per_sample, n_samples - 1), 0, 0)),
            pl.BlockSpec((None, 8, mix_a), lambda i: (e, 0, 0)),
            pl.BlockSpec((None, len(POOL_WINDOWS), pool_ch, pool_ch), lambda i: (e, 0, 0, 0)),
            pl.BlockSpec((None, 1, mix_b), lambda i: (e, 0, 0)),
            pl.BlockSpec((None, mix_a + mix_b, d), lambda i: (e, 0, 0), pipeline_mode=pl.Buffered(1)),
        ],
        out_specs=pl.BlockSpec((tm, d), lambda i: (i, 0)),
        out_shape=jax.ShapeDtypeStruct((n, d), F32),
        scratch_shapes=[
            pltpu.VMEM((tm + 2 * SUBLANES, mix_a), F32),
            pltpu.VMEM((npad, pool_ch), F32),
            pltpu.VMEM((npad, pool_ch), F32),
            pltpu.VMEM((tm, mix_a + mix_b), BF16),
        ],
        compiler_params=_cparams("arbitrary"),
        name="even_mixer",
    )(p2, p2, p2, x2, prm, conv_w8, pool_w, pool_scale, w_out)


def _prenorm_kernel(x_ref, prm_ref, o_ref, r_ref, *, rc):
    _rms_stats(x_ref, r_ref, rc)
    _pre_apply(x_ref, r_ref, prm_ref[0], o_ref, rc)


def _prenorm(x2, prm, rows_per_sample, *, tm):
    n, d = x2.shape
    tm = min(tm, rows_per_sample)
    tiles_per_sample = rows_per_sample // tm
    n_samples = prm.shape[0]
    return pl.pallas_call(
        functools.partial(_prenorm_kernel, rc=128),
        grid=(n // tm,),
        in_specs=[
            pl.BlockSpec((tm, d), lambda i: (i, 0)),
            pl.BlockSpec((1, 8, d), lambda i: (jnp.minimum(i // tiles_per_sample, n_samples - 1), 0, 0)),
        ],
        out_specs=pl.BlockSpec((tm, d), lambda i: (i, 0)),
        out_shape=jax.ShapeDtypeStruct((n, d), F32),
        scratch_shapes=[pltpu.VMEM((tm, LANES), F32)],
        compiler_params=_cparams("arbitrary"),
        name="s5_prenorm",
    )(x2, prm)


def _s5_drive_pieces(j, hc_ref, hl_ref, use_ctx, wb_ref, slab_ref, hbf_ref, buf_ref, *, tc, ch_sub, st_sub, npiece):
    nb = hl_ref.shape[0]
    pitch = tc + SUBLANES
    ch = slice(j * ch_sub, (j + 1) * ch_sub)

    def relayout():
        for k in range(j * ch_sub // LANES, (j + 1) * ch_sub // LANES):
            cols = slice(k * LANES, (k + 1) * LANES)
            for b in range(nb):
                slab_ref[k, b * pitch:b * pitch + tc, :] = jnp.where(use_ctx, hc_ref[b, :, cols], hl_ref[b, :, cols])
            for t in range(0, tc, 2):
                pair = [slab_ref.at[k][pl.ds(t + i, nb, stride=pitch), :] for i in range(2)]
                hbf_ref[t * nb:(t + 2) * nb, cols] = jnp.concatenate(pair, axis=0).astype(BF16)

    def matmul(p):
        w = st_sub // npiece
        cols = slice(j * st_sub + p * w, j * st_sub + (p + 1) * w)
        buf_ref[:, cols] = _dot(hbf_ref[:, ch], wb_ref[0, j, :, p * w:(p + 1) * w])

    return [relayout] + [functools.partial(matmul, p) for p in range(npiece)]


def _s5_readout_pieces(j, wc_ref, buf_ref, slab_ref, o_ref, *, tc, ch_sub, st_sub, npiece):
    nb = o_ref.shape[0]
    ch = slice(j * ch_sub, (j + 1) * ch_sub)

    def matmul(p):
        m = tc * nb // npiece
        rows = slice(p * m, (p + 1) * m)
        y = _dot(buf_ref[rows, j * st_sub:(j + 1) * st_sub], wc_ref[0, j])
        for i, k in enumerate(range(j * ch_sub // LANES, (j + 1) * ch_sub // LANES)):
            slab_ref[k, rows, :] = y[:, i * LANES:(i + 1) * LANES]

    def relayout():
        for k in range(j * ch_sub // LANES, (j + 1) * ch_sub // LANES):
            cols = slice(k * LANES, (k + 1) * LANES)
            for b in range(nb):
                o_ref[b, :, cols] = slab_ref.at[k][pl.ds(b, tc, stride=nb), :]

    return [functools.partial(matmul, p) for p in range(npiece)] + [relayout]


def _s5_scan_steps(j, sr, si, ar, ai, steps, buf_ref, sbf_ref, *, reverse, tc, st_sub, nb):
    half = st_sub // 2
    lo = j * st_sub
    assert len(steps) % 2 == 0 and steps[0] % 2 == 0
    for s0 in steps[::2]:
        pair = []
        for s in (s0, s0 + 1):
            t = (tc - 1 - s) if reverse else s
            rows = slice(t * nb, (t + 1) * nb)
            nr = ar * sr - ai * si + buf_ref[rows, lo:lo + half]
            ni = ar * si + ai * sr + buf_ref[rows, lo + half:lo + st_sub]
            sr, si = nr, ni
            pair.append((t, nr, ni))
        pair.sort(key=lambda e: e[0])
        rows2 = slice(pair[0][0] * nb, (pair[0][0] + 2) * nb)
        sbf_ref[rows2, lo:lo + half] = jnp.concatenate([pair[0][1], pair[1][1]], axis=0).astype(BF16)
        sbf_ref[rows2, lo + half:lo + st_sub] = jnp.concatenate([pair[0][2], pair[1][2]], axis=0).astype(BF16)
    return sr, si


def _s5_kernel(hc_ref, hl_ref, are_ref, aim_ref, wb_ref, wc_ref, o_ref,
               slab_in_ref, slab_out_ref, hbf_ref, buf0_ref, buf1_ref, sbf0_ref, sbf1_ref, st_ref,
               *, reverse, n_ctx_chunks, tc, nsub, ch_sub, st_sub):
    s = pl.program_id(1)
    nb = st_ref.shape[0]
    half = st_sub // 2
    bufs = (buf0_ref, buf1_ref)
    sbfs = (sbf0_ref, sbf1_ref)

    @pl.when(s == 0)
    def _():
        for r in bufs + sbfs:
            r[...] = jnp.zeros_like(r)
        st_ref[...] = jnp.zeros_like(st_ref)

    use_ctx = s < n_ctx_chunks
    for r in range(2):
        @pl.when(s % 2 == r)
        def _(r=r):
            drive_buf, scan_buf, state_buf, read_buf = bufs[r], bufs[1 - r], sbfs[1 - r], sbfs[r]
            scan_kw = dict(reverse=reverse, tc=tc, st_sub=st_sub, nb=nb)
            for j in range(nsub):
                re_cols = slice(2 * j * half, (2 * j + 1) * half)
                im_cols = slice((2 * j + 1) * half, (2 * j + 2) * half)
                ar = are_ref[0, :, j * half:(j + 1) * half]
                ai = aim_ref[0, :, j * half:(j + 1) * half]
                sr, si = st_ref[:, re_cols], st_ref[:, im_cols]
                dp = _s5_drive_pieces(j, hc_ref, hl_ref, use_ctx, wb_ref, slab_in_ref, hbf_ref, drive_buf,
                                      tc=tc, ch_sub=ch_sub, st_sub=st_sub, npiece=4)
                rp = _s5_readout_pieces(j, wc_ref, read_buf, slab_out_ref, o_ref,
                                        tc=tc, ch_sub=ch_sub, st_sub=st_sub, npiece=2)
                pieces = [dp[0], rp[0], dp[1], dp[2], rp[1], dp[3], dp[4], rp[2]]
                per = tc // len(pieces)
                for g, piece in enumerate(pieces):
                    last = tc if g == len(pieces) - 1 else (g + 1) * per
                    sr, si = _s5_scan_steps(j, sr, si, ar, ai, range(g * per, last), scan_buf, state_buf, **scan_kw)
                    piece()
                st_ref[:, re_cols] = sr
                st_ref[:, im_cols] = si


def _s5_scan(h_ctx3, h_lat3, a_re, a_im, wb, wc, *, reverse, tc):
    nb, lc, d = h_ctx3.shape
    l = h_lat3.shape[1]
    nq, nsub, ch_sub, st_sub = wb.shape
    dq = nsub * ch_sub
    tc = min(tc, lc)
    ncc, nlc = lc // tc, l // tc
    pitch = tc + SUBLANES

    n_chunks = ncc + nlc

    def ctx_idx(k):
        cc = jnp.clip(k, 0, ncc - 1)
        return (ncc - 1 - cc) if reverse else cc

    def lat_idx(k):
        lc_ = jnp.clip(k - ncc, 0, nlc - 1)
        return (nlc - 1 - lc_) if reverse else lc_

    kern = functools.partial(_s5_kernel, reverse=reverse, n_ctx_chunks=ncc, tc=tc, nsub=nsub, ch_sub=ch_sub, st_sub=st_sub)
    drive_buf = pltpu.VMEM((tc * nb, nsub * st_sub), F32)
    state_buf = pltpu.VMEM((tc * nb, nsub * st_sub), BF16)
    return pl.pallas_call(
        kern,
        grid=(nq, n_chunks + 2),
        in_specs=[
            pl.BlockSpec((nb, tc, dq), lambda q, s: (0, ctx_idx(jnp.minimum(s, n_chunks - 1)), q)),
            pl.BlockSpec((nb, tc, dq), lambda q, s: (0, lat_idx(jnp.minimum(s, n_chunks - 1)), q)),
            pl.BlockSpec((1, nb, nsub * st_sub // 2), lambda q, s: (q, 0, 0)),
            pl.BlockSpec((1, nb, nsub * st_sub // 2), lambda q, s: (q, 0, 0)),
            pl.BlockSpec((1, nsub, ch_sub, st_sub), lambda q, s: (q, 0, 0, 0)),
            pl.BlockSpec((1, nsub, st_sub, ch_sub), lambda q, s: (q, 0, 0, 0)),
        ],
        out_specs=pl.BlockSpec((nb, tc, dq), lambda q, s: (0, lat_idx(s - 2), q)),
        out_shape=jax.ShapeDtypeStruct((nb, l, d), F32),
        scratch_shapes=[
            pltpu.VMEM((dq // LANES, nb * pitch, LANES), F32),
            pltpu.VMEM((dq // LANES, nb * tc, LANES), F32),
            pltpu.VMEM((tc * nb, dq), BF16),
            drive_buf, drive_buf,
            state_buf, state_buf,
            pltpu.VMEM((nb, nsub * st_sub), F32),
        ],
        compiler_params=_cparams("arbitrary", "arbitrary"),
        name="s5_scan_rev" if reverse else "s5_scan_fwd",
    )(h_ctx3, h_lat3, a_re, a_im, wb, wc)


def _glu_kernel(x_ref, yf_ref, yr_ref, prm_ref, dv_ref, wa_ref, wb_ref, o_ref, *, rc):
    tm = x_ref.shape[0]
    prm = prm_ref[0]
    for c in range(tm // rc):
        rows = slice(c * rc, (c + 1) * rc)
        x = x_ref[rows, :]
        h = _pre_rows(x, prm)
        y = h * dv_ref[...] + yf_ref[rows, :] + yr_ref[rows, :]
        gy = jax.nn.gelu(y).astype(BF16)
        z = _dot(gy, wa_ref[...]) * jax.nn.sigmoid(_dot(gy, wb_ref[...]))
        o_ref[rows, :] = _post_rows(x, z, prm, 1.0)


def _glu(x2, yf2, yr2, prm, rows_per_sample, dv, wa, wb, o, *, tm, rc):
    n, d = x2.shape
    tm = min(tm, rows_per_sample)
    rc = min(rc, tm)
    tiles_per_sample = rows_per_sample // tm
    n_samples = prm.shape[0]
    return pl.pallas_call(
        functools.partial(_glu_kernel, rc=rc),
        grid=(n // tm,),
        in_specs=[
            pl.BlockSpec((tm, d), lambda i: (i, 0)),
            pl.BlockSpec((tm, d), lambda i: (i, 0)),
            pl.BlockSpec((tm, d), lambda i: (i, 0)),
            pl.BlockSpec((1, 8, d), lambda i: (jnp.minimum(i // tiles_per_sample, n_samples - 1), 0, 0)),
            pl.BlockSpec((None, 1, d), lambda i: (o, 0, 0)),
            pl.BlockSpec((None, d, d), lambda i: (o, 0, 0), pipeline_mode=pl.Buffered(1)),
            pl.BlockSpec((None, d, d), lambda i: (o, 0, 0), pipeline_mode=pl.Buffered(1)),
        ],
        out_specs=pl.BlockSpec((tm, d), lambda i: (i, 0)),
        out_shape=jax.ShapeDtypeStruct((n, d), F32),
        compiler_params=_cparams("arbitrary"),
        name="s5_glu",
    )(x2, yf2, yr2, prm, dv, wa, wb)


def _param_pack(m, g, i):
    s, _, d = m.shape
    rows = [m[:, 3 * i], m[:, 3 * i + 1], m[:, 3 * i + 2],
            jnp.broadcast_to(g[2 * i], (s, d)), jnp.broadcast_to(g[2 * i + 1], (s, d))]
    rows += [jnp.zeros((s, d), F32)] * 3
    return jnp.stack(rows, axis=1)


def _s5_discretise_kernel(lre_ref, lim_ref, ls_ref, bre_ref, bim_ref, are_ref, aim_ref, bbre_ref, bbim_ref):
    lam_re, lam_im = lre_ref[...], lim_ref[...]
    dt = jnp.exp(ls_ref[...])
    mag = jnp.exp(lam_re * dt)
    lbar_re = mag * jnp.cos(lam_im * dt)
    lbar_im = mag * jnp.sin(lam_im * dt)
    den = lam_re * lam_re + lam_im * lam_im
    nr = lbar_re - 1.0
    fr = (nr * lam_re + lbar_im * lam_im) / den
    fi = (lbar_im * lam_re - nr * lam_im) / den
    b_re, b_im = bre_ref[...], bim_ref[...]
    are_ref[...] = lbar_re
    aim_ref[...] = lbar_im
    bbre_ref[...] = fr * b_re - fi * b_im
    bbim_ref[...] = fr * b_im + fi * b_re


def _s5_discretise(lam_re, lam_im, log_step, b_re, b_im):
    nd, g, p, h = b_re.shape
    flat = (nd * g, p * h)
    rep = lambda a: jnp.broadcast_to(a[..., None], (nd, g, p, h)).reshape(flat)
    ls = jnp.broadcast_to(log_step[..., None, None], (nd, g, p, h)).reshape(flat)
    outs = pl.pallas_call(
        _s5_discretise_kernel,
        out_shape=[jax.ShapeDtypeStruct(flat, F32)] * 4,
        name="s5_discretise",
    )(rep(lam_re), rep(lam_im), ls, b_re.reshape(flat), b_im.reshape(flat))
    lbar_re, lbar_im, bbar_re, bbar_im = [o.reshape(nd, g, p, h) for o in outs]
    return lbar_re[..., 0], lbar_im[..., 0], bbar_re, bbar_im


def _s5_layout(lbar_re, lbar_im, bbar_re, bbar_im, c_re, c_im, nb, gsub, nsub):
    g, p, h = bbar_re.shape
    nblk = g // gsub
    eye = jnp.eye(gsub, dtype=F32)

    def drive(bb):
        bb = bb.reshape(nblk, gsub, p, h)
        return jnp.einsum('ngph,gk->nghkp', bb, eye).reshape(nblk, gsub * h, gsub * p)

    def read(cc):
        cc = cc.reshape(nblk, gsub, h, p)
        return jnp.einsum('nghp,gk->ngpkh', cc, eye).reshape(nblk, gsub * p, gsub * h)

    wb = jnp.concatenate([drive(bbar_re), drive(bbar_im)], axis=2)
    wc = jnp.concatenate([read(c_re), -read(c_im)], axis=1)
    nq = nblk // nsub
    wb = wb.reshape(nq, nsub, gsub * h, 2 * gsub * p).astype(BF16)
    wc = wc.reshape(nq, nsub, 2 * gsub * p, gsub * h).astype(BF16)
    a_re = jnp.broadcast_to(lbar_re.reshape(nq, 1, nsub * gsub * p), (nq, nb, nsub * gsub * p))
    a_im = jnp.broadcast_to(lbar_im.reshape(nq, 1, nsub * gsub * p), (nq, nb, nsub * gsub * p))
    return a_re, a_im, wb, wc


def kernel(x, c, ctx, c_ctx, w_mod, b_mod, norm_g, ffn_w_gate, ffn_w_up, ffn_w_down, mix_in, conv_w, pool_w, pool_scale, mix_out, s5_lambda_re, s5_lambda_im, s5_log_step, s5_b_re, s5_b_im, s5_c_re, s5_c_im, s5_d, glu_w_a, glu_w_b):
    b, l, d = x.shape
    lc = ctx.shape[1]
    depth = w_mod.shape[0]
    assert b == SUBLANES, "the S5 recurrence keeps the batch on the 8 sublanes"
    assert l % GRID_W == 0 and d % LANES == 0

    wg, wu, wd = ffn_w_gate.astype(BF16), ffn_w_up.astype(BF16), ffn_w_down.astype(BF16)
    w_in, w_out = mix_in.astype(BF16), mix_out.astype(BF16)
    pw = pool_w.astype(BF16)
    wa, wbb = glu_w_a.astype(BF16), glu_w_b.astype(BF16)
    conv_w8 = jnp.pad(conv_w, ((0, 0), (0, 8 - conv_w.shape[1]), (0, 0)))
    pscale = pool_scale[:, None, :]

    c_all = jnp.concatenate([c, c_ctx[None], jnp.zeros((2 * SUBLANES - b - 1, d), F32)], axis=0)
    m_all = _modulation(c_all, w_mod, b_mod)[:, :b + 1].reshape(depth, b + 1, N_MOD, d)

    x_lat = x.reshape(b * l, d)
    x_ctx = ctx.reshape(b * lc, d)
    ffn_lat = ffn_ctx = functools.partial(_ffn, tm=FFN_ROWS, tf=FFN_FF_COLS)

    for layer in range(depth):
        last = layer == depth - 1
        even = layer % 2 == 0
        run_ctx_in = not (last and even)
        ctx_out = not last
        g = norm_g[layer]
        m_lat, m_ctx = m_all[layer, :b], m_all[layer, b:]
        pk_lat = [_param_pack(m_lat, g, i) for i in range(3)]
        pk_ctx = [_param_pack(m_ctx, g, i) for i in range(3)]

        x_lat = ffn_lat(x_lat, pk_lat[0], l, wg, wu, wd, layer, 0)
        if run_ctx_in:
            x_ctx = ffn_ctx(x_ctx, pk_ctx[0], b * lc, wg, wu, wd, layer, 0)

        if even:
            e = layer // 2
            p_lat = _preproj(x_lat, pk_lat[1], l, w_in, e, tm=PROJ_ROWS, tn=PROJ_COLS)
            x_lat_new = _evenmix(p_lat, x_lat, pk_lat[1], l, conv_w8, pw, pscale, w_out, e, tm=MIX_ROWS, blk=GRID_W)
            if ctx_out:
                p_ctx = _preproj(x_ctx, pk_ctx[1], b * lc, w_in, e, tm=PROJ_ROWS, tn=PROJ_COLS)
                x_ctx = _evenmix(p_ctx, x_ctx, pk_ctx[1], lc, conv_w8, pw, pscale, w_out, e, tm=lc, blk=lc)
            x_lat = x_lat_new
        else:
            o = layer // 2
            h_lat = _prenorm(x_lat, pk_lat[1], l, tm=NORM_ROWS)
            h_ctx = _prenorm(x_ctx, pk_ctx[1], b * lc, tm=NORM_ROWS)
            lbr, lbi, bbr, bbi = _s5_discretise(s5_lambda_re[o], s5_lambda_im[o], s5_log_step[o], s5_b_re[o], s5_b_im[o])
            n_groups = lbr.shape[1]
            gsub = S5_GROUPS_PER_SUBBLOCK
            nsub = min(S5_SUBBLOCKS_PER_STEP, n_groups // gsub)
            ys = []
            for dr, rev in ((0, False), (1, True)):
                a_re, a_im, wb_s5, wc_s5 = _s5_layout(lbr[dr], lbi[dr], bbr[dr], bbi[dr], s5_c_re[o, dr], s5_c_im[o, dr],
                                                      b, gsub, nsub)
                ys.append(_s5_scan(h_ctx.reshape(b, lc, d), h_lat.reshape(b, l, d), a_re, a_im, wb_s5, wc_s5,
                                   reverse=rev, tc=S5_TIME_CHUNK))
            x_lat = _glu(x_lat, ys[0].reshape(b * l, d), ys[1].reshape(b * l, d), pk_lat[1], l, s5_d[:, None, :], wa, wbb, o,
                         tm=GLU_ROWS, rc=GLU_CHUNK)
            assert not ctx_out, "context output of an S5 layer is not needed by this block"

        x_lat = ffn_lat(x_lat, pk_lat[2], l, wg, wu, wd, layer, 1)
        if ctx_out:
            x_ctx = ffn_ctx(x_ctx, pk_ctx[2], b * lc, wg, wu, wd, layer, 1)

    return x_lat.reshape(b, l, d)
```

```python
import functools

import jax
import jax.numpy as jnp
import numpy as np
from jax import lax
from jax.experimental import pallas as pl
from jax.experimental.pallas import tpu as pltpu

EPS = 1e-6
GRID_W = 64
POOL_WINDOWS = (2, 4, 8, 16)
N_MOD = 9

F32 = jnp.float32
BF16 = jnp.bfloat16

LANES = 128
SUBLANES = 8
VMEM_LIMIT_BYTES = 56 * 1024 * 1024
FFN_ROWS, FFN_FF_COLS = 1024, 512
FFN_EDGE_CHUNK = 256
PROJ_ROWS, PROJ_COLS = 1024, 1024
MIX_ROWS, MIX_CHUNK = 512, 512
NORM_ROWS = 1024
GLU_ROWS, GLU_CHUNK = 512, 256
S5_TIME_CHUNK = 64
S5_GROUPS_PER_SUBBLOCK = SUBLANES
S5_SUBBLOCKS_PER_STEP = 4
ROW_SHIFT, ROW_SCALE, ROW_GATE, ROW_GPRE, ROW_GPOST = 0, 1, 2, 3, 4


def _cparams(*sem, flags=None):
    return pltpu.CompilerParams(dimension_semantics=sem, vmem_limit_bytes=VMEM_LIMIT_BYTES, flags=flags)


def _dot(a, b):
    return jnp.dot(a, b, preferred_element_type=F32)


def _pre_rows(x, prm):
    r = lax.rsqrt(jnp.mean(x * x, axis=-1, keepdims=True) + EPS)
    return (x * r * prm[ROW_GPRE:ROW_GPRE + 1]) * (1.0 + prm[ROW_SCALE:ROW_SCALE + 1]) + prm[ROW_SHIFT:ROW_SHIFT + 1]


def _post_rows(x, y, prm, weight):
    r = lax.rsqrt(jnp.mean(y * y, axis=-1, keepdims=True) + EPS)
    return x + (weight * prm[ROW_GATE:ROW_GATE + 1]) * (y * r * prm[ROW_GPOST:ROW_GPOST + 1])


def _rms_stats(src_ref, r_ref, rc, rows=None):
    tm, d = src_ref.shape

    def body(rows):
        v = src_ref[rows, :]
        ss = jnp.sum(v * v, axis=-1, keepdims=True)
        r_ref[rows, :] = jnp.broadcast_to(lax.rsqrt(ss / d + EPS), (rows.size, LANES))
    _row_chunks(tm, rc, body, unroll=2, rows=rows)


def _pre_apply(x_ref, r_ref, prm, h_ref, rc, zero_ref=None, rows=None):
    tm, d = x_ref.shape

    def body(rows):
        r = r_ref[rows, :]
        for k in range(d // LANES):
            cols = slice(k * LANES, (k + 1) * LANES)
            h = (x_ref[rows, cols] * r * prm[ROW_GPRE:ROW_GPRE + 1, cols]) * (1.0 + prm[ROW_SCALE:ROW_SCALE + 1, cols])
            h_ref[rows, cols] = (h + prm[ROW_SHIFT:ROW_SHIFT + 1, cols]).astype(h_ref.dtype)
            if zero_ref is not None:
                zero_ref[rows, cols] = jnp.zeros((rows.size, LANES), zero_ref.dtype)
    _row_chunks(tm, rc, body, rows=rows)


def _post_apply(x_ref, y_ref, r_ref, prm, o_ref, weight, rc, rows=None):
    tm, d = x_ref.shape

    def body(rows):
        r = r_ref[rows, :]
        for k in range(d // LANES):
            cols = slice(k * LANES, (k + 1) * LANES)
            yn = y_ref[rows, cols] * r * prm[ROW_GPOST:ROW_GPOST + 1, cols]
            o_ref[rows, cols] = x_ref[rows, cols] + (weight * prm[ROW_GATE:ROW_GATE + 1, cols]) * yn
    _row_chunks(tm, rc, body, rows=rows)


def _pick_tile(total, target, quantum=LANES):
    if total <= target:
        return total
    best = quantum
    for t in range(quantum, target + 1, quantum):
        if total % t == 0:
            best = t
    assert total % best == 0, (total, target, quantum)
    return best


def _row_chunks(total, chunk, body, unroll=1, rows=None):
    if rows is not None:
        start, size = rows
        chunk = min(chunk, size)
        for k in range(size // chunk):
            body(pl.ds(start + k * chunk, chunk))
        return
    chunk = min(chunk, total)

    def step(k, carry):
        body(pl.ds(pl.multiple_of(k * chunk, chunk), chunk))
        return carry

    lax.fori_loop(0, total // chunk, step, 0, unroll=unroll)


def _mod_kernel(c_ref, w_ref, b_ref, o_ref):
    sc = jax.nn.silu(c_ref[...]).astype(BF16)
    o_ref[0] = _dot(sc, w_ref[0].astype(BF16)) + b_ref[0]


def _modulation(c_all, w_mod, b_mod):
    depth, d, nd = w_mod.shape
    rows = c_all.shape[0]
    tn = _pick_tile(nd, 1024)
    return pl.pallas_call(
        _mod_kernel,
        grid=(depth, nd // tn),
        in_specs=[
            pl.BlockSpec((rows, d), lambda l, j: (0, 0)),
            pl.BlockSpec((1, d, tn), lambda l, j: (l, 0, j)),
            pl.BlockSpec((1, 1, tn), lambda l, j: (l, 0, j)),
        ],
        out_specs=pl.BlockSpec((1, rows, tn), lambda l, j: (l, 0, j)),
        out_shape=jax.ShapeDtypeStruct((depth, rows, nd), F32),
        compiler_params=_cparams("arbitrary", "arbitrary"),
        name="modulation",
    )(c_all, w_mod, b_mod.reshape(depth, 1, nd))


def _ffn_kernel(x_ref, prm_ref, wg_ref, wu_ref, wd_ref, o_ref, h_ref, r_ref, *, nj, rc, mc):
    j = pl.program_id(1)
    prm = prm_ref[0]
    tm = x_ref.shape[0]

    def swiglu_down(rows):
        h = h_ref[rows, :]
        t = (jax.nn.silu(_dot(h, wg_ref[...])) * _dot(h, wu_ref[...])).astype(BF16)
        return _dot(t, wd_ref[...])

    @pl.when(j == 0)
    def _():
        for c in range(tm // mc):
            _rms_stats(x_ref, r_ref, rc, rows=(c * mc, mc))
            _pre_apply(x_ref, r_ref, prm, h_ref, rc, rows=(c * mc, mc))
            o_ref[c * mc:(c + 1) * mc, :] = swiglu_down(slice(c * mc, (c + 1) * mc))

    @pl.when(jnp.logical_and(j > 0, j < nj - 1))
    def _():
        o_ref[...] += swiglu_down(slice(None))

    @pl.when(j == nj - 1)
    def _():
        for c in range(tm // mc):
            o_ref[c * mc:(c + 1) * mc, :] += swiglu_down(slice(c * mc, (c + 1) * mc))
            _rms_stats(o_ref, r_ref, rc, rows=(c * mc, mc))
            _post_apply(x_ref, o_ref, r_ref, prm, o_ref, 0.5, rc, rows=(c * mc, mc))


def _ffn(x2, prm, rows_per_sample, wg, wu, wd, l, k, *, tm, tf):
    n, d = x2.shape
    f = wg.shape[-1]
    tm = min(tm, rows_per_sample)
    tf = _pick_tile(f, tf)
    nj = f // tf
    tiles_per_sample = max(rows_per_sample // tm, 1)
    n_samples = prm.shape[0]
    assert nj >= 2, "the first and the last d_ff step are distinct code paths"
    kern = functools.partial(_ffn_kernel, nj=nj, rc=128, mc=min(FFN_EDGE_CHUNK, tm))
    return pl.pallas_call(
        kern,
        grid=(n // tm, nj),
        in_specs=[
            pl.BlockSpec((tm, d), lambda i, j: (i, 0)),
            pl.BlockSpec((1, 8, d), lambda i, j: (jnp.minimum(i // tiles_per_sample, n_samples - 1), 0, 0)),
            pl.BlockSpec((None, None, d, tf), lambda i, j: (l, k, 0, j)),
            pl.BlockSpec((None, None, d, tf), lambda i, j: (l, k, 0, j)),
            pl.BlockSpec((None, None, tf, d), lambda i, j: (l, k, j, 0)),
        ],
        out_specs=pl.BlockSpec((tm, d), lambda i, j: (i, 0)),
        out_shape=jax.ShapeDtypeStruct((n, d), F32),
        scratch_shapes=[pltpu.VMEM((tm, d), BF16), pltpu.VMEM((tm, LANES), F32)],
        compiler_params=_cparams("arbitrary", "arbitrary"),
        name="ffn",
    )(x2, prm, wg, wu, wd)


def _preproj_kernel(x_ref, prm_ref, w_ref, o_ref, h_ref, r_ref, *, rc, mc):
    j = pl.program_id(1)
    tm = x_ref.shape[0]

    @pl.when(j == 0)
    def _():
        for c in range(tm // mc):
            _rms_stats(x_ref, r_ref, rc, rows=(c * mc, mc))
            _pre_apply(x_ref, r_ref, prm_ref[0], h_ref, rc, rows=(c * mc, mc))
            o_ref[c * mc:(c + 1) * mc, :] = _dot(h_ref[c * mc:(c + 1) * mc, :], w_ref[...])

    @pl.when(j > 0)
    def _():
        o_ref[...] = _dot(h_ref[...], w_ref[...])


def _preproj(x2, prm, rows_per_sample, w, e, *, tm, tn):
    n, d = x2.shape
    nc = w.shape[-1]
    tm = min(tm, rows_per_sample)
    tn = _pick_tile(nc, tn)
    tiles_per_sample = max(rows_per_sample // tm, 1)
    n_samples = prm.shape[0]
    return pl.pallas_call(
        functools.partial(_preproj_kernel, rc=128, mc=min(FFN_EDGE_CHUNK, tm)),
        grid=(n // tm, nc // tn),
        in_specs=[
            pl.BlockSpec((tm, d), lambda i, j: (i, 0)),
            pl.BlockSpec((1, 8, d), lambda i, j: (jnp.minimum(i // tiles_per_sample, n_samples - 1), 0, 0)),
            pl.BlockSpec((None, d, tn), lambda i, j: (e, 0, j)),
        ],
        out_specs=pl.BlockSpec((tm, tn), lambda i, j: (i, j)),
        out_shape=jax.ShapeDtypeStruct((n, nc), F32),
        scratch_shapes=[pltpu.VMEM((tm, d), BF16), pltpu.VMEM((tm, LANES), F32)],
        compiler_params=_cparams("arbitrary", "arbitrary"),
        name="mix_in_proj",
    )(x2, prm, w)


def _evenmix_kernel(p_ref, pprev_ref, pnext_ref, x_ref, prm_ref, cw_ref, pw_ref, ps_ref, wo_ref, o_ref,
                    cv_ref, pa_ref, pb_ref, ycat_ref, *, blk, mc, tiles_per_sample, mix_a, pool_ch):
    i = pl.program_id(0)
    tm = x_ref.shape[0]
    prm = prm_ref[0]
    nblk = mc // blk
    pitch = blk + 2 * SUBLANES
    npad = nblk * pitch + SUBLANES
    inner = npad - 2 * SUBLANES
    t_in_sample = i % tiles_per_sample
    has_prev = (t_in_sample != 0).astype(F32)
    has_next = (t_in_sample != tiles_per_sample - 1).astype(F32)
    zeros8 = jnp.zeros((SUBLANES, pool_ch), F32)

    c0, c1, c2, c3 = 0, mix_a, 2 * mix_a, 3 * mix_a
    cv_ref[SUBLANES:SUBLANES + tm, :] = p_ref[:, c1:c2] * p_ref[:, c2:c3]
    cv_ref[0:SUBLANES, :] = pprev_ref[:, c1:c2] * pprev_ref[:, c2:c3] * has_prev
    cv_ref[SUBLANES + tm:2 * SUBLANES + tm, :] = pnext_ref[:, c1:c2] * pnext_ref[:, c2:c3] * has_next
    pos = lax.broadcasted_iota(jnp.int32, (mc, 1), 0) % blk
    pb_ref[0:SUBLANES, :] = zeros8
    pb_ref[npad - SUBLANES:npad, :] = zeros8

    for c in range(tm // mc):
        r0 = c * mc
        rows = slice(r0, r0 + mc)
        conv = (cw_ref[0:1, :] * cv_ref[SUBLANES - 1 + r0:SUBLANES - 1 + r0 + mc, :]
                + cw_ref[1:2, :] * cv_ref[SUBLANES + r0:SUBLANES + r0 + mc, :]
                + cw_ref[2:3, :] * cv_ref[SUBLANES + 1 + r0:SUBLANES + 1 + r0 + mc, :])
        ycat_ref[rows, 0:mix_a] = (p_ref[rows, c0:c1] * conv).astype(BF16)

        for gi, w in enumerate(POOL_WINDOWS):
            lo, hi = c3 + gi * pool_ch, c3 + (gi + 1) * pool_ch
            pa_ref[0:SUBLANES, :] = zeros8
            for b in range(nblk):
                base = b * pitch + SUBLANES
                pa_ref[base:base + blk, :] = p_ref[r0 + b * blk:r0 + (b + 1) * blk, lo:hi]
                pa_ref[base + blk:base + blk + SUBLANES, :] = zeros8
                pa_ref[base + blk + SUBLANES:base + blk + 2 * SUBLANES, :] = zeros8
            offsets = [(SUBLANES - 1, SUBLANES)]
            half = 1
            while 4 * half <= w:
                offsets.append((SUBLANES - half, SUBLANES + half))
                half *= 2
            src, dst = pa_ref, pb_ref
            for lo_off, hi_off in offsets:
                dst[SUBLANES:SUBLANES + inner, :] = src[lo_off:lo_off + inner, :] + src[hi_off:hi_off + inner, :]
                src, dst = dst, src
            cnt = (jnp.minimum(pos + w // 2, blk) - jnp.maximum(pos - w // 2, 0)).astype(F32)
            pieces = [src[b * pitch + SUBLANES:b * pitch + SUBLANES + blk, :] for b in range(nblk)]
            total = pieces[0] if nblk == 1 else jnp.concatenate(pieces, axis=0)
            pooled = total / cnt - p_ref[rows, lo:hi]
            yb = _dot(pooled.astype(BF16), pw_ref[gi]) * ps_ref[:, gi * pool_ch:(gi + 1) * pool_ch]
            ycat_ref[rows, mix_a + gi * pool_ch:mix_a + (gi + 1) * pool_ch] = yb.astype(BF16)

        y = _dot(ycat_ref[rows, :], wo_ref[...])
        o_ref[rows, :] = _post_rows(x_ref[rows, :], y, prm, 1.0)


def _evenmix(p2, x2, prm, rows_per_sample, conv_w8, pool_w, pool_scale, w_out, e, *, tm, blk):
    n, d = x2.shape
    nc = p2.shape[1]
    mix_a = conv_w8.shape[-1]
    pool_ch = pool_w.shape[-1]
    mix_b = pool_ch * len(POOL_WINDOWS)
    tm = min(tm, rows_per_sample)
    tiles_per_sample = rows_per_sample // tm
    n_samples = prm.shape[0]
    mc = max(blk, min(MIX_CHUNK, tm))
    assert tm % mc == 0 and mc % blk == 0
    npad = (mc // blk) * (blk + 2 * SUBLANES) + SUBLANES
    tb = tm // SUBLANES
    nb8 = n // SUBLANES
    kern = functools.partial(_evenmix_kernel, blk=blk, mc=mc, tiles_per_sample=tiles_per_sample, mix_a=mix_a,
                             pool_ch=pool_ch)
    return pl.pallas_call(
        kern,
        grid=(n // tm,),
        in_specs=[
            pl.BlockSpec((tm, nc), lambda i: (i, 0)),
            pl.BlockSpec((SUBLANES, nc), lambda i: (jnp.maximum(i * tb - 1, 0), 0)),
            pl.BlockSpec((SUBLANES, nc), lambda i: (jnp.minimum((i + 1) * tb, nb8 - 1), 0)),
            pl.BlockSpec((tm, d), lambda i: (i, 0)),
            pl.BlockSpec((1, 8, d), lambda i: (jnp.minimum(i // tiles_per_sample, n_samples - 1), 0, 0)),
            pl.BlockSpec((None, 8, mix_a), lambda i: (e, 0, 0)),
            pl.BlockSpec((None, len(POOL_WINDOWS), pool_ch, pool_ch), lambda i: (e, 0, 0, 0)),
            pl.BlockSpec((None, 1, mix_b), lambda i: (e, 0, 0)),
            pl.BlockSpec((None, mix_a + mix_b, d), lambda i: (e, 0, 0), pipeline_mode=pl.Buffered(1)),
        ],
        out_specs=pl.BlockSpec((tm, d), lambda i: (i, 0)),
        out_shape=jax.ShapeDtypeStruct((n, d), F32),
        scratch_shapes=[
            pltpu.VMEM((tm + 2 * SUBLANES, mix_a), F32),
            pltpu.VMEM((npad, pool_ch), F32),
            pltpu.VMEM((npad, pool_ch), F32),
            pltpu.VMEM((tm, mix_a + mix_b), BF16),
        ],
        compiler_params=_cparams("arbitrary"),
        name="even_mixer",
    )(p2, p2, p2, x2, prm, conv_w8, pool_w, pool_scale, w_out)


def _prenorm_kernel(x_ref, prm_ref, o_ref, r_ref, *, rc):
    _rms_stats(x_ref, r_ref, rc)
    _pre_apply(x_ref, r_ref, prm_ref[0], o_ref, rc)


def _prenorm(x2, prm, rows_per_sample, *, tm):
    n, d = x2.shape
    tm = min(tm, rows_per_sample)
    tiles_per_sample = rows_per_sample // tm
    n_samples = prm.shape[0]
    return pl.pallas_call(
        functools.partial(_prenorm_kernel, rc=128),
        grid=(n // tm,),
        in_specs=[
            pl.BlockSpec((tm, d), lambda i: (i, 0)),
            pl.BlockSpec((1, 8, d), lambda i: (jnp.minimum(i // tiles_per_sample, n_samples - 1), 0, 0)),
        ],
        out_specs=pl.BlockSpec((tm, d), lambda i: (i, 0)),
        out_shape=jax.ShapeDtypeStruct((n, d), F32),
        scratch_shapes=[pltpu.VMEM((tm, LANES), F32)],
        compiler_params=_cparams("arbitrary"),
        name="s5_prenorm",
    )(x2, prm)


def _s5_drive_pieces(j, hc_ref, hl_ref, use_ctx, wb_ref, slab_ref, hbf_ref, buf_ref, *, tc, ch_sub, st_sub, npiece):
    nb = hl_ref.shape[0]
    pitch = tc + SUBLANES
    ch = slice(j * ch_sub, (j + 1) * ch_sub)

    def relayout():
        for k in range(j * ch_sub // LANES, (j + 1) * ch_sub // LANES):
            cols = slice(k * LANES, (k + 1) * LANES)
            for b in range(nb):
                slab_ref[k, b * pitch:b * pitch + tc, :] = jnp.where(use_ctx, hc_ref[b, :, cols], hl_ref[b, :, cols])
            for t in range(0, tc, 2):
                pair = [slab_ref.at[k][pl.ds(t + i, nb, stride=pitch), :] for i in range(2)]
                hbf_ref[t * nb:(t + 2) * nb, cols] = jnp.concatenate(pair, axis=0).astype(BF16)

    def matmul(p):
        w = st_sub // npiece
        cols = slice(j * st_sub + p * w, j * st_sub + (p + 1) * w)
        buf_ref[:, cols] = _dot(hbf_ref[:, ch], wb_ref[0, j, :, p * w:(p + 1) * w])

    return [relayout] + [functools.partial(matmul, p) for p in range(npiece)]


def _s5_readout_pieces(j, wc_ref, buf_ref, slab_ref, o_ref, *, tc, ch_sub, st_sub, npiece):
    nb = o_ref.shape[0]
    ch = slice(j * ch_sub, (j + 1) * ch_sub)

    def matmul(p):
        m = tc * nb // npiece
        rows = slice(p * m, (p + 1) * m)
        y = _dot(buf_ref[rows, j * st_sub:(j + 1) * st_sub], wc_ref[0, j])
        for i, k in enumerate(range(j * ch_sub // LANES, (j + 1) * ch_sub // LANES)):
            slab_ref[k, rows, :] = y[:, i * LANES:(i + 1) * LANES]

    def relayout():
        for k in range(j * ch_sub // LANES, (j + 1) * ch_sub // LANES):
            cols = slice(k * LANES, (k + 1) * LANES)
            for b in range(nb):
                o_ref[b, :, cols] = slab_ref.at[k][pl.ds(b, tc, stride=nb), :]

    return [functools.partial(matmul, p) for p in range(npiece)] + [relayout]


def _s5_scan_steps(j, sr, si, ar, ai, steps, buf_ref, sbf_ref, *, reverse, tc, st_sub, nb):
    half = st_sub // 2
    lo = j * st_sub
    assert len(steps) % 2 == 0 and steps[0] % 2 == 0
    for s0 in steps[::2]:
        pair = []
        for s in (s0, s0 + 1):
            t = (tc - 1 - s) if reverse else s
            rows = slice(t * nb, (t + 1) * nb)
            nr = ar * sr - ai * si + buf_ref[rows, lo:lo + half]
            ni = ar * si + ai * sr + buf_ref[rows, lo + half:lo + st_sub]
            sr, si = nr, ni
            pair.append((t, nr, ni))
        pair.sort(key=lambda e: e[0])
        rows2 = slice(pair[0][0] * nb, (pair[0][0] + 2) * nb)
        sbf_ref[rows2, lo:lo + half] = jnp.concatenate([pair[0][1], pair[1][1]], axis=0).astype(BF16)
        sbf_ref[rows2, lo + half:lo + st_sub] = jnp.concatenate([pair[0][2], pair[1][2]], axis=0).astype(BF16)
    return sr, si


def _s5_kernel(hc_ref, hl_ref, are_ref, aim_ref, wb_ref, wc_ref, o_ref,
               slab_in_ref, slab_out_ref, hbf_ref, buf0_ref, buf1_ref, sbf0_ref, sbf1_ref, st_ref,
               *, reverse, n_ctx_chunks, tc, nsub, ch_sub, st_sub):
    s = pl.program_id(1)
    nb = st_ref.shape[0]
    half = st_sub // 2
    bufs = (buf0_ref, buf1_ref)
    sbfs = (sbf0_ref, sbf1_ref)

    @pl.when(s == 0)
    def _():
        for r in bufs + sbfs:
            r[...] = jnp.zeros_like(r)
        st_ref[...] = jnp.zeros_like(st_ref)

    use_ctx = s < n_ctx_chunks
    for r in range(2):
        @pl.when(s % 2 == r)
        def _(r=r):
            drive_buf, scan_buf, state_buf, read_buf = bufs[r], bufs[1 - r], sbfs[1 - r], sbfs[r]
            scan_kw = dict(reverse=reverse, tc=tc, st_sub=st_sub, nb=nb)
            for j in range(nsub):
                re_cols = slice(2 * j * half, (2 * j + 1) * half)
                im_cols = slice((2 * j + 1) * half, (2 * j + 2) * half)
                ar = are_ref[0, :, j * half:(j + 1) * half]
                ai = aim_ref[0, :, j * half:(j + 1) * half]
                sr, si = st_ref[:, re_cols], st_ref[:, im_cols]
                dp = _s5_drive_pieces(j, hc_ref, hl_ref, use_ctx, wb_ref, slab_in_ref, hbf_ref, drive_buf,
                                      tc=tc, ch_sub=ch_sub, st_sub=st_sub, npiece=4)
                rp = _s5_readout_pieces(j, wc_ref, read_buf, slab_out_ref, o_ref,
                                        tc=tc, ch_sub=ch_sub, st_sub=st_sub, npiece=2)
                pieces = [dp[0], rp[0], dp[1], dp[2], rp[1], dp[3], dp[4], rp[2]]
                per = tc // len(pieces)
                for g, piece in enumerate(pieces):
                    last = tc if g == len(pieces) - 1 else (g + 1) * per
                    sr, si = _s5_scan_steps(j, sr, si, ar, ai, range(g * per, last), scan_buf, state_buf, **scan_kw)
                    piece()
                st_ref[:, re_cols] = sr
                st_ref[:, im_cols] = si


def _s5_scan(h_ctx3, h_lat3, a_re, a_im, wb, wc, *, reverse, tc):
    nb, lc, d = h_ctx3.shape
    l = h_lat3.shape[1]
    nq, nsub, ch_sub, st_sub = wb.shape
    dq = nsub * ch_sub
    tc = min(tc, lc)
    ncc, nlc = lc // tc, l // tc
    pitch = tc + SUBLANES

    n_chunks = ncc + nlc

    def ctx_idx(k):
        cc = jnp.clip(k, 0, ncc - 1)
        return (ncc - 1 - cc) if reverse else cc

    def lat_idx(k):
        lc_ = jnp.clip(k - ncc, 0, nlc - 1)
        return (nlc - 1 - lc_) if reverse else lc_

    kern = functools.partial(_s5_kernel, reverse=reverse, n_ctx_chunks=ncc, tc=tc, nsub=nsub, ch_sub=ch_sub, st_sub=st_sub)
    drive_buf = pltpu.VMEM((tc * nb, nsub * st_sub), F32)
    state_buf = pltpu.VMEM((tc * nb, nsub * st_sub), BF16)
    return pl.pallas_call(
        kern,
        grid=(nq, n_chunks + 2),
        in_specs=[
            pl.BlockSpec((nb, tc, dq), lambda q, s: (0, ctx_idx(jnp.minimum(s, n_chunks - 1)), q)),
            pl.BlockSpec((nb, tc, dq), lambda q, s: (0, lat_idx(jnp.minimum(s, n_chunks - 1)), q)),
            pl.BlockSpec((1, nb, nsub * st_sub // 2), lambda q, s: (q, 0, 0)),
            pl.BlockSpec((1, nb, nsub * st_sub // 2), lambda q, s: (q, 0, 0)),
            pl.BlockSpec((1, nsub, ch_sub, st_sub), lambda q, s: (q, 0, 0, 0)),
            pl.BlockSpec((1, nsub, st_sub, ch_sub), lambda q, s: (q, 0, 0, 0)),
        ],
        out_specs=pl.BlockSpec((nb, tc, dq), lambda q, s: (0, lat_idx(s - 2), q)),
        out_shape=jax.ShapeDtypeStruct((nb, l, d), F32),
        scratch_shapes=[
            pltpu.VMEM((dq // LANES, nb * pitch, LANES), F32),
            pltpu.VMEM((dq // LANES, nb * tc, LANES), F32),
            pltpu.VMEM((tc * nb, dq), BF16),
            drive_buf, drive_buf,
            state_buf, state_buf,
            pltpu.VMEM((nb, nsub * st_sub), F32),
        ],
        compiler_params=_cparams("arbitrary", "arbitrary"),
        name="s5_scan_rev" if reverse else "s5_scan_fwd",
    )(h_ctx3, h_lat3, a_re, a_im, wb, wc)


def _glu_kernel(x_ref, yf_ref, yr_ref, prm_ref, dv_ref, wa_ref, wb_ref, o_ref, *, rc):
    tm = x_ref.shape[0]
    prm = prm_ref[0]
    for c in range(tm // rc):
        rows = slice(c * rc, (c + 1) * rc)
        x = x_ref[rows, :]
        h = _pre_rows(x, prm)
        y = h * dv_ref[...] + yf_ref[rows, :] + yr_ref[rows, :]
        gy = jax.nn.gelu(y).astype(BF16)
        z = _dot(gy, wa_ref[...]) * jax.nn.sigmoid(_dot(gy, wb_ref[...]))
        o_ref[rows, :] = _post_rows(x, z, prm, 1.0)


def _glu(x2, yf2, yr2, prm, rows_per_sample, dv, wa, wb, o, *, tm, rc):
    n, d = x2.shape
    tm = min(tm, rows_per_sample)
    rc = min(rc, tm)
    tiles_per_sample = rows_per_sample // tm
    n_samples = prm.shape[0]
    return pl.pallas_call(
        functools.partial(_glu_kernel, rc=rc),
        grid=(n // tm,),
        in_specs=[
            pl.BlockSpec((tm, d), lambda i: (i, 0)),
            pl.BlockSpec((tm, d), lambda i: (i, 0)),
            pl.BlockSpec((tm, d), lambda i: (i, 0)),
            pl.BlockSpec((1, 8, d), lambda i: (jnp.minimum(i // tiles_per_sample, n_samples - 1), 0, 0)),
            pl.BlockSpec((None, 1, d), lambda i: (o, 0, 0)),
            pl.BlockSpec((None, d, d), lambda i: (o, 0, 0), pipeline_mode=pl.Buffered(1)),
            pl.BlockSpec((None, d, d), lambda i: (o, 0, 0), pipeline_mode=pl.Buffered(1)),
        ],
        out_specs=pl.BlockSpec((tm, d), lambda i: (i, 0)),
        out_shape=jax.ShapeDtypeStruct((n, d), F32),
        compiler_params=_cparams("arbitrary"),
        name="s5_glu",
    )(x2, yf2, yr2, prm, dv, wa, wb)


def _param_pack(m, g, i):
    s, _, d = m.shape
    rows = [m[:, 3 * i], m[:, 3 * i + 1], m[:, 3 * i + 2],
            jnp.broadcast_to(g[2 * i], (s, d)), jnp.broadcast_to(g[2 * i + 1], (s, d))]
    rows += [jnp.zeros((s, d), F32)] * 3
    return jnp.stack(rows, axis=1)


def _s5_discretise_kernel(lre_ref, lim_ref, ls_ref, bre_ref, bim_ref, are_ref, aim_ref, bbre_ref, bbim_ref):
    lam_re, lam_im = lre_ref[...], lim_ref[...]
    dt = jnp.exp(ls_ref[...])
    mag = jnp.exp(lam_re * dt)
    lbar_re = mag * jnp.cos(lam_im * dt)
    lbar_im = mag * jnp.sin(lam_im * dt)
    den = lam_re * lam_re + lam_im * lam_im
    nr = lbar_re - 1.0
    fr = (nr * lam_re + lbar_im * lam_im) / den
    fi = (lbar_im * lam_re - nr * lam_im) / den
    b_re, b_im = bre_ref[...], bim_ref[...]
    are_ref[...] = lbar_re
    aim_ref[...] = lbar_im
    bbre_ref[...] = fr * b_re - fi * b_im
    bbim_ref[...] = fr * b_im + fi * b_re


def _s5_discretise(lam_re, lam_im, log_step, b_re, b_im):
    nd, g, p, h = b_re.shape
    flat = (nd * g, p * h)
    rep = lambda a: jnp.broadcast_to(a[..., None], (nd, g, p, h)).reshape(flat)
    ls = jnp.broadcast_to(log_step[..., None, None], (nd, g, p, h)).reshape(flat)
    outs = pl.pallas_call(
        _s5_discretise_kernel,
        out_shape=[jax.ShapeDtypeStruct(flat, F32)] * 4,
        name="s5_discretise",
    )(rep(lam_re), rep(lam_im), ls, b_re.reshape(flat), b_im.reshape(flat))
    lbar_re, lbar_im, bbar_re, bbar_im = [o.reshape(nd, g, p, h) for o in outs]
    return lbar_re[..., 0], lbar_im[..., 0], bbar_re, bbar_im


def _s5_layout(lbar_re, lbar_im, bbar_re, bbar_im, c_re, c_im, nb, gsub, nsub):
    g, p, h = bbar_re.shape
    nblk = g // gsub
    eye = jnp.eye(gsub, dtype=F32)

    def drive(bb):
        bb = bb.reshape(nblk, gsub, p, h)
        return jnp.einsum('ngph,gk->nghkp', bb, eye).reshape(nblk, gsub * h, gsub * p)

    def read(cc):
        cc = cc.reshape(nblk, gsub, h, p)
        return jnp.einsum('nghp,gk->ngpkh', cc, eye).reshape(nblk, gsub * p, gsub * h)

    wb = jnp.concatenate([drive(bbar_re), drive(bbar_im)], axis=2)
    wc = jnp.concatenate([read(c_re), -read(c_im)], axis=1)
    nq = nblk // nsub
    wb = wb.reshape(nq, nsub, gsub * h, 2 * gsub * p).astype(BF16)
    wc = wc.reshape(nq, nsub, 2 * gsub * p, gsub * h).astype(BF16)
    a_re = jnp.broadcast_to(lbar_re.reshape(nq, 1, nsub * gsub * p), (nq, nb, nsub * gsub * p))
    a_im = jnp.broadcast_to(lbar_im.reshape(nq, 1, nsub * gsub * p), (nq, nb, nsub * gsub * p))
    return a_re, a_im, wb, wc


def kernel(x, c, ctx, c_ctx, w_mod, b_mod, norm_g, ffn_w_gate, ffn_w_up, ffn_w_down, mix_in, conv_w, pool_w, pool_scale, mix_out, s5_lambda_re, s5_lambda_im, s5_log_step, s5_b_re, s5_b_im, s5_c_re, s5_c_im, s5_d, glu_w_a, glu_w_b):
    b, l, d = x.shape
    lc = ctx.shape[1]
    depth = w_mod.shape[0]
    assert b == SUBLANES, "the S5 recurrence keeps the batch on the 8 sublanes"
    assert l % GRID_W == 0 and d % LANES == 0

    wg, wu, wd = ffn_w_gate.astype(BF16), ffn_w_up.astype(BF16), ffn_w_down.astype(BF16)
    w_in, w_out = mix_in.astype(BF16), mix_out.astype(BF16)
    pw = pool_w.astype(BF16)
    wa, wbb = glu_w_a.astype(BF16), glu_w_b.astype(BF16)
    conv_w8 = jnp.pad(conv_w, ((0, 0), (0, 8 - conv_w.shape[1]), (0, 0)))
    pscale = pool_scale[:, None, :]

    c_all = jnp.concatenate([c, c_ctx[None], jnp.zeros((2 * SUBLANES - b - 1, d), F32)], axis=0)
    m_all = _modulation(c_all, w_mod, b_mod)[:, :b + 1].reshape(depth, b + 1, N_MOD, d)

    x_lat = x.reshape(b * l, d)
    x_ctx = ctx.reshape(b * lc, d)
    ffn_lat = ffn_ctx = functools.partial(_ffn, tm=FFN_ROWS, tf=FFN_FF_COLS)

    for layer in range(depth):
        last = layer == depth - 1
        even = layer % 2 == 0
        run_ctx_in = not (last and even)
        ctx_out = not last
        g = norm_g[layer]
        m_lat, m_ctx = m_all[layer, :b], m_all[layer, b:]
        pk_lat = [_param_pack(m_lat, g, i) for i in range(3)]
        pk_ctx = [_param_pack(m_ctx, g, i) for i in range(3)]

        x_lat = ffn_lat(x_lat, pk_lat[0], l, wg, wu, wd, layer, 0)
        if run_ctx_in:
            x_ctx = ffn_ctx(x_ctx, pk_ctx[0], b * lc, wg, wu, wd, layer, 0)

        if even:
            e = layer // 2
            p_lat = _preproj(x_lat, pk_lat[1], l, w_in, e, tm=PROJ_ROWS, tn=PROJ_COLS)
            x_lat_new = _evenmix(p_lat, x_lat, pk_lat[1], l, conv_w8, pw, pscale, w_out, e, tm=MIX_ROWS, blk=GRID_W)
            if ctx_out:
                p_ctx = _preproj(x_ctx, pk_ctx[1], b * lc, w_in, e, tm=PROJ_ROWS, tn=PROJ_COLS)
                x_ctx = _evenmix(p_ctx, x_ctx, pk_ctx[1], lc, conv_w8, pw, pscale, w_out, e, tm=lc, blk=lc)
            x_lat = x_lat_new
        else:
            o = layer // 2
            h_lat = _prenorm(x_lat, pk_lat[1], l, tm=NORM_ROWS)
            h_ctx = _prenorm(x_ctx, pk_ctx[1], b * lc, tm=NORM_ROWS)
            lbr, lbi, bbr, bbi = _s5_discretise(s5_lambda_re[o], s5_lambda_im[o], s5_log_step[o], s5_b_re[o], s5_b_im[o])
            n_groups = lbr.shape[1]
            gsub = S5_GROUPS_PER_SUBBLOCK
            nsub = min(S5_SUBBLOCKS_PER_STEP, n_groups // gsub)
            ys = []
            for dr, rev in ((0, False), (1, True)):
                a_re, a_im, wb_s5, wc_s5 = _s5_layout(lbr[dr], lbi[dr], bbr[dr], bbi[dr], s5_c_re[o, dr], s5_c_im[o, dr],
                                                      b, gsub, nsub)
                ys.append(_s5_scan(h_ctx.reshape(b, lc, d), h_lat.reshape(b, l, d), a_re, a_im, wb_s5, wc_s5,
                                   reverse=rev, tc=S5_TIME_CHUNK))
            x_lat = _glu(x_lat, ys[0].reshape(b * l, d), ys[1].reshape(b * l, d), pk_lat[1], l, s5_d[:, None, :], wa, wbb, o,
                         tm=GLU_ROWS, rc=GLU_CHUNK)
            assert not ctx_out, "context output of an S5 layer is not needed by this block"

        x_lat = ffn_lat(x_lat, pk_lat[2], l, wg, wu, wd, layer, 1)
        if ctx_out:
            x_ctx = ffn_ctx(x_ctx, pk_ctx[2], b * lc, wg, wu, wd, layer, 1)

    return x_lat.reshape(b, l, d)
```

```python
import functools

import jax
import jax.numpy as jnp
from jax import lax
from jax.experimental import pallas as pl
from jax.experimental.pallas import tpu as pltpu

EPS = 1e-6
GRID_W = 64
POOL_WINDOWS = (2, 4, 8, 16)
N_MOD = 9

F32 = jnp.float32
BF16 = jnp.bfloat16

LANES = 128
SUBLANES = 8
VMEM_LIMIT_BYTES = 56 * 1024 * 1024
FFN_ROWS, FFN_FF_COLS = 1024, 512
FFN_EDGE_CHUNK = 256
PROJ_ROWS, PROJ_COLS = 1024, 1024
MIX_ROWS, MIX_CHUNK = 512, 512
NORM_ROWS = 1024
GLU_ROWS, GLU_CHUNK = 512, 256
S5_TIME_CHUNK = 64
S5_GROUPS_PER_SUBBLOCK = SUBLANES
S5_SUBBLOCKS_PER_STEP = 4
ROW_SHIFT, ROW_SCALE, ROW_GATE, ROW_GPRE, ROW_GPOST = 0, 1, 2, 3, 4


def _cparams(*sem):
    return pltpu.CompilerParams(dimension_semantics=sem, vmem_limit_bytes=VMEM_LIMIT_BYTES)


def _dot(a, b):
    return jnp.dot(a, b, preferred_element_type=F32)


def _pre_rows(x, prm):
    r = lax.rsqrt(jnp.mean(x * x, axis=-1, keepdims=True) + EPS)
    return (x * r * prm[ROW_GPRE:ROW_GPRE + 1]) * (1.0 + prm[ROW_SCALE:ROW_SCALE + 1]) + prm[ROW_SHIFT:ROW_SHIFT + 1]


def _post_rows(x, y, prm, weight):
    r = lax.rsqrt(jnp.mean(y * y, axis=-1, keepdims=True) + EPS)
    return x + (weight * prm[ROW_GATE:ROW_GATE + 1]) * (y * r * prm[ROW_GPOST:ROW_GPOST + 1])


def _rms_stats(src_ref, r_ref, rc, rows=None):
    tm, d = src_ref.shape

    def body(rows):
        v = src_ref[rows, :]
        ss = jnp.sum(v * v, axis=-1, keepdims=True)
        r_ref[rows, :] = jnp.broadcast_to(lax.rsqrt(ss / d + EPS), (rows.size, LANES))
    _row_chunks(tm, rc, body, unroll=2, rows=rows)


def _pre_apply(x_ref, r_ref, prm, h_ref, rc, rows=None):
    tm, d = x_ref.shape

    def body(rows):
        r = r_ref[rows, :]
        for k in range(d // LANES):
            cols = slice(k * LANES, (k + 1) * LANES)
            h = (x_ref[rows, cols] * r * prm[ROW_GPRE:ROW_GPRE + 1, cols]) * (1.0 + prm[ROW_SCALE:ROW_SCALE + 1, cols])
            h_ref[rows, cols] = (h + prm[ROW_SHIFT:ROW_SHIFT + 1, cols]).astype(h_ref.dtype)
    _row_chunks(tm, rc, body, rows=rows)


def _post_apply(x_ref, y_ref, r_ref, prm, o_ref, weight, rc, rows=None):
    tm, d = x_ref.shape

    def body(rows):
        r = r_ref[rows, :]
        for k in range(d // LANES):
            cols = slice(k * LANES, (k + 1) * LANES)
            yn = y_ref[rows, cols] * r * prm[ROW_GPOST:ROW_GPOST + 1, cols]
            o_ref[rows, cols] = x_ref[rows, cols] + (weight * prm[ROW_GATE:ROW_GATE + 1, cols]) * yn
    _row_chunks(tm, rc, body, rows=rows)


def _pick_tile(total, target, quantum=LANES):
    if total <= target:
        return total
    best = quantum
    for t in range(quantum, target + 1, quantum):
        if total % t == 0:
            best = t
    assert total % best == 0, (total, target, quantum)
    return best


def _row_chunks(total, chunk, body, unroll=1, rows=None):
    if rows is not None:
        start, size = rows
        chunk = min(chunk, size)
        for k in range(size // chunk):
            body(pl.ds(start + k * chunk, chunk))
        return
    chunk = min(chunk, total)

    def step(k, carry):
        body(pl.ds(pl.multiple_of(k * chunk, chunk), chunk))
        return carry

    lax.fori_loop(0, total // chunk, step, 0, unroll=unroll)


def _mod_kernel(c_ref, w_ref, b_ref, o_ref):
    sc = jax.nn.silu(c_ref[...]).astype(BF16)
    o_ref[0] = _dot(sc, w_ref[0].astype(BF16)) + b_ref[0]


def _modulation(c_all, w_mod, b_mod):
    depth, d, nd = w_mod.shape
    rows = c_all.shape[0]
    tn = _pick_tile(nd, 1024)
    return pl.pallas_call(
        _mod_kernel,
        grid=(depth, nd // tn),
        in_specs=[
            pl.BlockSpec((rows, d), lambda l, j: (0, 0)),
            pl.BlockSpec((1, d, tn), lambda l, j: (l, 0, j)),
            pl.BlockSpec((1, 1, tn), lambda l, j: (l, 0, j)),
        ],
        out_specs=pl.BlockSpec((1, rows, tn), lambda l, j: (l, 0, j)),
        out_shape=jax.ShapeDtypeStruct((depth, rows, nd), F32),
        compiler_params=_cparams("arbitrary", "arbitrary"),
        name="modulation",
    )(c_all, w_mod, b_mod.reshape(depth, 1, nd))


def _ffn_kernel(x_ref, prm_ref, wg_ref, wu_ref, wd_ref, o_ref, h_ref, r_ref, *, nj, rc, mc):
    j = pl.program_id(1)
    prm = prm_ref[0]
    tm = x_ref.shape[0]

    def swiglu_down(rows):
        h = h_ref[rows, :]
        t = (jax.nn.silu(_dot(h, wg_ref[...])) * _dot(h, wu_ref[...])).astype(BF16)
        return _dot(t, wd_ref[...])

    @pl.when(j == 0)
    def _():
        for c in range(tm // mc):
            _rms_stats(x_ref, r_ref, rc, rows=(c * mc, mc))
            _pre_apply(x_ref, r_ref, prm, h_ref, rc, rows=(c * mc, mc))
            o_ref[c * mc:(c + 1) * mc, :] = swiglu_down(slice(c * mc, (c + 1) * mc))

    @pl.when(jnp.logical_and(j > 0, j < nj - 1))
    def _():
        o_ref[...] += swiglu_down(slice(None))

    @pl.when(j == nj - 1)
    def _():
        for c in range(tm // mc):
            o_ref[c * mc:(c + 1) * mc, :] += swiglu_down(slice(c * mc, (c + 1) * mc))
            _rms_stats(o_ref, r_ref, rc, rows=(c * mc, mc))
            _post_apply(x_ref, o_ref, r_ref, prm, o_ref, 0.5, rc, rows=(c * mc, mc))


def _ffn(x2, prm, rows_per_sample, wg, wu, wd, l, k, *, tm, tf):
    n, d = x2.shape
    f = wg.shape[-1]
    tm = min(tm, rows_per_sample)
    tf = _pick_tile(f, tf)
    nj = f // tf
    tiles_per_sample = max(rows_per_sample // tm, 1)
    n_samples = prm.shape[0]
    assert nj >= 2, "the first and the last d_ff step are distinct code paths"
    kern = functools.partial(_ffn_kernel, nj=nj, rc=128, mc=min(FFN_EDGE_CHUNK, tm))
    return pl.pallas_call(
        kern,
        grid=(n // tm, nj),
        in_specs=[
            pl.BlockSpec((tm, d), lambda i, j: (i, 0)),
            pl.BlockSpec((1, 8, d), lambda i, j: (jnp.minimum(i // tiles_per_sample, n_samples - 1), 0, 0)),
            pl.BlockSpec((None, None, d, tf), lambda i, j: (l, k, 0, j)),
            pl.BlockSpec((None, None, d, tf), lambda i, j: (l, k, 0, j)),
            pl.BlockSpec((None, None, tf, d), lambda i, j: (l, k, j, 0)),
        ],
        out_specs=pl.BlockSpec((tm, d), lambda i, j: (i, 0)),
        out_shape=jax.ShapeDtypeStruct((n, d), F32),
        scratch_shapes=[pltpu.VMEM((tm, d), BF16), pltpu.VMEM((tm, LANES), F32)],
        compiler_params=_cparams("arbitrary", "arbitrary"),
        name="ffn",
    )(x2, prm, wg, wu, wd)


def _preproj_kernel(x_ref, prm_ref, w_ref, o_ref, h_ref, r_ref, *, rc, mc):
    j = pl.program_id(1)
    tm = x_ref.shape[0]

    @pl.when(j == 0)
    def _():
        for c in range(tm // mc):
            _rms_stats(x_ref, r_ref, rc, rows=(c * mc, mc))
            _pre_apply(x_ref, r_ref, prm_ref[0], h_ref, rc, rows=(c * mc, mc))
            o_ref[c * mc:(c + 1) * mc, :] = _dot(h_ref[c * mc:(c + 1) * mc, :], w_ref[...])

    @pl.when(j > 0)
    def _():
        o_ref[...] = _dot(h_ref[...], w_ref[...])


def _preproj(x2, prm, rows_per_sample, w, e, *, tm, tn):
    n, d = x2.shape
    nc = w.shape[-1]
    tm = min(tm, rows_per_sample)
    tn = _pick_tile(nc, tn)
    tiles_per_sample = max(rows_per_sample // tm, 1)
    n_samples = prm.shape[0]
    return pl.pallas_call(
        functools.partial(_preproj_kernel, rc=128, mc=min(FFN_EDGE_CHUNK, tm)),
        grid=(n // tm, nc // tn),
        in_specs=[
            pl.BlockSpec((tm, d), lambda i, j: (i, 0)),
            pl.BlockSpec((1, 8, d), lambda i, j: (jnp.minimum(i // tiles_per_sample, n_samples - 1), 0, 0)),
            pl.BlockSpec((None, d, tn), lambda i, j: (e, 0, j)),
        ],
        out_specs=pl.BlockSpec((tm, tn), lambda i, j: (i, j)),
        out_shape=jax.ShapeDtypeStruct((n, nc), F32),
        scratch_shapes=[pltpu.VMEM((tm, d), BF16), pltpu.VMEM((tm, LANES), F32)],
        compiler_params=_cparams("arbitrary", "arbitrary"),
        name="mix_in_proj",
    )(x2, prm, w)


def _evenmix_kernel(p_ref, pprev_ref, pnext_ref, x_ref, prm_ref, cw_ref, pw_ref, ps_ref, wo_ref, o_ref,
                    cv_ref, pa_ref, pb_ref, ycat_ref, *, blk, mc, tiles_per_sample, mix_a, pool_ch):
    i = pl.program_id(0)
    tm = x_ref.shape[0]
    prm = prm_ref[0]
    nblk = mc // blk
    pitch = blk + 2 * SUBLANES
    npad = nblk * pitch + SUBLANES
    inner = npad - 2 * SUBLANES
    t_in_sample = i % tiles_per_sample
    has_prev = (t_in_sample != 0).astype(F32)
    has_next = (t_in_sample != tiles_per_sample - 1).astype(F32)
    zeros8 = jnp.zeros((SUBLANES, pool_ch), F32)

    c0, c1, c2, c3 = 0, mix_a, 2 * mix_a, 3 * mix_a
    cv_ref[SUBLANES:SUBLANES + tm, :] = p_ref[:, c1:c2] * p_ref[:, c2:c3]
    cv_ref[0:SUBLANES, :] = pprev_ref[:, c1:c2] * pprev_ref[:, c2:c3] * has_prev
    cv_ref[SUBLANES + tm:2 * SUBLANES + tm, :] = pnext_ref[:, c1:c2] * pnext_ref[:, c2:c3] * has_next
    pos = lax.broadcasted_iota(jnp.int32, (mc, 1), 0) % blk
    pb_ref[0:SUBLANES, :] = zeros8
    pb_ref[npad - SUBLANES:npad, :] = zeros8

    for c in range(tm // mc):
        r0 = c * mc
        rows = slice(r0, r0 + mc)
        conv = (cw_ref[0:1, :] * cv_ref[SUBLANES - 1 + r0:SUBLANES - 1 + r0 + mc, :]
                + cw_ref[1:2, :] * cv_ref[SUBLANES + r0:SUBLANES + r0 + mc, :]
                + cw_ref[2:3, :] * cv_ref[SUBLANES + 1 + r0:SUBLANES + 1 + r0 + mc, :])
        ycat_ref[rows, 0:mix_a] = (p_ref[rows, c0:c1] * conv).astype(BF16)

        for gi, w in enumerate(POOL_WINDOWS):
            lo, hi = c3 + gi * pool_ch, c3 + (gi + 1) * pool_ch
            pa_ref[0:SUBLANES, :] = zeros8
            for b in range(nblk):
                base = b * pitch + SUBLANES
                pa_ref[base:base + blk, :] = p_ref[r0 + b * blk:r0 + (b + 1) * blk, lo:hi]
                pa_ref[base + blk:base + blk + SUBLANES, :] = zeros8
                pa_ref[base + blk + SUBLANES:base + blk + 2 * SUBLANES, :] = zeros8
            offsets = [(SUBLANES - 1, SUBLANES)]
            half = 1
            while 4 * half <= w:
                offsets.append((SUBLANES - half, SUBLANES + half))
                half *= 2
            src, dst = pa_ref, pb_ref
            for lo_off, hi_off in offsets:
                dst[SUBLANES:SUBLANES + inner, :] = src[lo_off:lo_off + inner, :] + src[hi_off:hi_off + inner, :]
                src, dst = dst, src
            cnt = (jnp.minimum(pos + w // 2, blk) - jnp.maximum(pos - w // 2, 0)).astype(F32)
            pieces = [src[b * pitch + SUBLANES:b * pitch + SUBLANES + blk, :] for b in range(nblk)]
            total = pieces[0] if nblk == 1 else jnp.concatenate(pieces, axis=0)
            pooled = total / cnt - p_ref[rows, lo:hi]
            yb = _dot(pooled.astype(BF16), pw_ref[gi]) * ps_ref[:, gi * pool_ch:(gi + 1) * pool_ch]
            ycat_ref[rows, mix_a + gi * pool_ch:mix_a + (gi + 1) * pool_ch] = yb.astype(BF16)

        y = _dot(ycat_ref[rows, :], wo_ref[...])
        o_ref[rows, :] = _post_rows(x_ref[rows, :], y, prm, 1.0)


def _evenmix(p2, x2, prm, rows_per_sample, conv_w8, pool_w, pool_scale, w_out, e, *, tm, blk):
    n, d = x2.shape
    nc = p2.shape[1]
    mix_a = conv_w8.shape[-1]
    pool_ch = pool_w.shape[-1]
    mix_b = pool_ch * len(POOL_WINDOWS)
    tm = min(tm, rows_per_sample)
    tiles_per_sample = rows_per_sample // tm
    n_samples = prm.shape[0]
    mc = max(blk, min(MIX_CHUNK, tm))
    assert tm % mc == 0 and mc % blk == 0
    npad = (mc // blk) * (blk + 2 * SUBLANES) + SUBLANES
    tb = tm // SUBLANES
    nb8 = n // SUBLANES
    kern = functools.partial(_evenmix_kernel, blk=blk, mc=mc, tiles_per_sample=tiles_per_sample, mix_a=mix_a,
                             pool_ch=pool_ch)
    return pl.pallas_call(
        kern,
        grid=(n // tm,),
        in_specs=[
            pl.BlockSpec((tm, nc), lambda i: (i, 0)),
            pl.BlockSpec((SUBLANES, nc), lambda i: (jnp.maximum(i * tb - 1, 0), 0)),
            pl.BlockSpec((SUBLANES, nc), lambda i: (jnp.minimum((i + 1) * tb, nb8 - 1), 0)),
            pl.BlockSpec((tm, d), lambda i: (i, 0)),
            pl.BlockSpec((1, 8, d), lambda i: (jnp.minimum(i // tiles_per_sample, n_samples - 1), 0, 0)),
            pl.BlockSpec((None, 8, mix_a), lambda i: (e, 0, 0)),
            pl.BlockSpec((None, len(POOL_WINDOWS), pool_ch, pool_ch), lambda i: (e, 0, 0, 0)),
            pl.BlockSpec((None, 1, mix_b), lambda i: (e, 0, 0)),
            pl.BlockSpec((None, mix_a + mix_b, d), lambda i: (e, 0, 0), pipeline_mode=pl.Buffered(1)),
        ],
        out_specs=pl.BlockSpec((tm, d), lambda i: (i, 0)),
        out_shape=jax.ShapeDtypeStruct((n, d), F32),
        scratch_shapes=[
            pltpu.VMEM((tm + 2 * SUBLANES, mix_a), F32),
            pltpu.VMEM((npad, pool_ch), F32),
            pltpu.VMEM((npad, pool_ch), F32),
            pltpu.VMEM((tm, mix_a + mix_b), BF16),
        ],
        compiler_params=_cparams("arbitrary"),
        name="even_mixer",
    )(p2, p2, p2, x2, prm, conv_w8, pool_w, pool_scale, w_out)


def _prenorm_kernel(x_ref, prm_ref, o_ref, r_ref, *, rc):
    _rms_stats(x_ref, r_ref, rc)
    _pre_apply(x_ref, r_ref, prm_ref[0], o_ref, rc)


def _prenorm(x2, prm, rows_per_sample, *, tm):
    n, d = x2.shape
    tm = min(tm, rows_per_sample)
    tiles_per_sample = rows_per_sample // tm
    n_samples = prm.shape[0]
    return pl.pallas_call(
        functools.partial(_prenorm_kernel, rc=128),
        grid=(n // tm,),
        in_specs=[
            pl.BlockSpec((tm, d), lambda i: (i, 0)),
            pl.BlockSpec((1, 8, d), lambda i: (jnp.minimum(i // tiles_per_sample, n_samples - 1), 0, 0)),
        ],
        out_specs=pl.BlockSpec((tm, d), lambda i: (i, 0)),
        out_shape=jax.ShapeDtypeStruct((n, d), BF16),
        scratch_shapes=[pltpu.VMEM((tm, LANES), F32)],
        compiler_params=_cparams("arbitrary"),
        name="s5_prenorm",
    )(x2, prm)


def _s5_drive_pieces(j, hc_ref, hl_ref, use_ctx, wb_ref, slab_ref, hbf_ref, buf_ref, *, tc, ch_sub, st_sub, npiece):
    nb = hl_ref.shape[0]
    pitch = tc + SUBLANES
    ch = slice(j * ch_sub, (j + 1) * ch_sub)

    def relayout():
        for k in range(j * ch_sub // LANES, (j + 1) * ch_sub // LANES):
            cols = slice(k * LANES, (k + 1) * LANES)
            for b in range(nb):
                hb = jnp.where(use_ctx, hc_ref[b, :, cols], hl_ref[b, :, cols])
                slab_ref[k, b * pitch:b * pitch + tc, :] = hb.astype(F32)
            for t in range(0, tc, 2):
                pair = [slab_ref.at[k][pl.ds(t + i, nb, stride=pitch), :] for i in range(2)]
                hbf_ref[t * nb:(t + 2) * nb, cols] = jnp.concatenate(pair, axis=0).astype(BF16)

    def matmul(p):
        w = st_sub // npiece
        cols = slice(j * st_sub + p * w, j * st_sub + (p + 1) * w)
        buf_ref[:, cols] = _dot(hbf_ref[:, ch], wb_ref[0, j, :, p * w:(p + 1) * w])

    return [relayout] + [functools.partial(matmul, p) for p in range(npiece)]


def _s5_readout_pieces(j, wc_ref, buf_ref, slab_ref, o_ref, *, tc, ch_sub, st_sub, npiece):
    nb = o_ref.shape[0]
    ch = slice(j * ch_sub, (j + 1) * ch_sub)

    def matmul(p):
        m = tc * nb // npiece
        rows = slice(p * m, (p + 1) * m)
        y = _dot(buf_ref[rows, j * st_sub:(j + 1) * st_sub], wc_ref[0, j])
        for i, k in enumerate(range(j * ch_sub // LANES, (j + 1) * ch_sub // LANES)):
            slab_ref[k, rows, :] = y[:, i * LANES:(i + 1) * LANES]

    def relayout():
        for k in range(j * ch_sub // LANES, (j + 1) * ch_sub // LANES):
            cols = slice(k * LANES, (k + 1) * LANES)
            for b in range(nb):
                o_ref[b, :, cols] = slab_ref.at[k][pl.ds(b, tc, stride=nb), :]

    return [functools.partial(matmul, p) for p in range(npiece)] + [relayout]


def _s5_scan_steps(j, sr, si, ar, ai, steps, buf_ref, sbf_ref, *, reverse, tc, st_sub, nb):
    half = st_sub // 2
    lo = j * st_sub
    assert len(steps) % 2 == 0 and steps[0] % 2 == 0
    for s0 in steps[::2]:
        pair = []
        for s in (s0, s0 + 1):
            t = (tc - 1 - s) if reverse else s
            rows = slice(t * nb, (t + 1) * nb)
            nr = ar * sr - ai * si + buf_ref[rows, lo:lo + half]
            ni = ar * si + ai * sr + buf_ref[rows, lo + half:lo + st_sub]
            sr, si = nr, ni
            pair.append((t, nr, ni))
        pair.sort(key=lambda e: e[0])
        rows2 = slice(pair[0][0] * nb, (pair[0][0] + 2) * nb)
        sbf_ref[rows2, lo:lo + half] = jnp.concatenate([pair[0][1], pair[1][1]], axis=0).astype(BF16)
        sbf_ref[rows2, lo + half:lo + st_sub] = jnp.concatenate([pair[0][2], pair[1][2]], axis=0).astype(BF16)
    return sr, si


def _s5_kernel(hc_ref, hl_ref, are_ref, aim_ref, wb_ref, wc_ref, o_ref,
               slab_in_ref, slab_out_ref, hbf_ref, buf0_ref, buf1_ref, sbf0_ref, sbf1_ref, st_ref,
               *, reverse, n_ctx_chunks, tc, nsub, ch_sub, st_sub):
    s = pl.program_id(1)
    nb = st_ref.shape[0]
    half = st_sub // 2
    bufs = (buf0_ref, buf1_ref)
    sbfs = (sbf0_ref, sbf1_ref)

    @pl.when(s == 0)
    def _():
        for r in bufs + sbfs:
            r[...] = jnp.zeros_like(r)
        st_ref[...] = jnp.zeros_like(st_ref)

    use_ctx = s < n_ctx_chunks
    for r in range(2):
        @pl.when(s % 2 == r)
        def _(r=r):
            drive_buf, scan_buf, state_buf, read_buf = bufs[r], bufs[1 - r], sbfs[1 - r], sbfs[r]
            scan_kw = dict(reverse=reverse, tc=tc, st_sub=st_sub, nb=nb)
            for j in range(nsub):
                re_cols = slice(2 * j * half, (2 * j + 1) * half)
                im_cols = slice((2 * j + 1) * half, (2 * j + 2) * half)
                ar = are_ref[0, :, j * half:(j + 1) * half]
                ai = aim_ref[0, :, j * half:(j + 1) * half]
                sr, si = st_ref[:, re_cols], st_ref[:, im_cols]
                dp = _s5_drive_pieces(j, hc_ref, hl_ref, use_ctx, wb_ref, slab_in_ref, hbf_ref, drive_buf,
                                      tc=tc, ch_sub=ch_sub, st_sub=st_sub, npiece=4)
                rp = _s5_readout_pieces(j, wc_ref, read_buf, slab_out_ref, o_ref,
                                        tc=tc, ch_sub=ch_sub, st_sub=st_sub, npiece=2)
                pieces = [dp[0], rp[0], dp[1], dp[2], rp[1], dp[3], dp[4], rp[2]]
                per = tc // len(pieces)
                for g, piece in enumerate(pieces):
                    last = tc if g == len(pieces) - 1 else (g + 1) * per
                    sr, si = _s5_scan_steps(j, sr, si, ar, ai, range(g * per, last), scan_buf, state_buf, **scan_kw)
                    piece()
                st_ref[:, re_cols] = sr
                st_ref[:, im_cols] = si


def _s5_scan(h_ctx3, h_lat3, a_re, a_im, wb, wc, *, reverse, tc):
    nb, lc, d = h_ctx3.shape
    l = h_lat3.shape[1]
    nq, nsub, ch_sub, st_sub = wb.shape
    dq = nsub * ch_sub
    tc = min(tc, lc)
    ncc, nlc = lc // tc, l // tc
    pitch = tc + SUBLANES

    n_chunks = ncc + nlc

    def ctx_idx(k):
        cc = jnp.clip(k, 0, ncc - 1)
        return (ncc - 1 - cc) if reverse else cc

    def lat_idx(k):
        lc_ = jnp.clip(k - ncc, 0, nlc - 1)
        return (nlc - 1 - lc_) if reverse else lc_

    kern = functools.partial(_s5_kernel, reverse=reverse, n_ctx_chunks=ncc, tc=tc, nsub=nsub, ch_sub=ch_sub, st_sub=st_sub)
    drive_buf = pltpu.VMEM((tc * nb, nsub * st_sub), F32)
    state_buf = pltpu.VMEM((tc * nb, nsub * st_sub), BF16)
    return pl.pallas_call(
        kern,
        grid=(nq, n_chunks + 2),
        in_specs=[
            pl.BlockSpec((nb, tc, dq), lambda q, s: (0, ctx_idx(jnp.minimum(s, n_chunks - 1)), q)),
            pl.BlockSpec((nb, tc, dq), lambda q, s: (0, lat_idx(jnp.minimum(s, n_chunks - 1)), q)),
            pl.BlockSpec((1, nb, nsub * st_sub // 2), lambda q, s: (q, 0, 0)),
            pl.BlockSpec((1, nb, nsub * st_sub // 2), lambda q, s: (q, 0, 0)),
            pl.BlockSpec((1, nsub, ch_sub, st_sub), lambda q, s: (q, 0, 0, 0)),
            pl.BlockSpec((1, nsub, st_sub, ch_sub), lambda q, s: (q, 0, 0, 0)),
        ],
        out_specs=pl.BlockSpec((nb, tc, dq), lambda q, s: (0, lat_idx(s - 2), q)),
        out_shape=jax.ShapeDtypeStruct((nb, l, d), F32),
        scratch_shapes=[
            pltpu.VMEM((dq // LANES, nb * pitch, LANES), F32),
            pltpu.VMEM((dq // LANES, nb * tc, LANES), F32),
            pltpu.VMEM((tc * nb, dq), BF16),
            drive_buf, drive_buf,
            state_buf, state_buf,
            pltpu.VMEM((nb, nsub * st_sub), F32),
        ],
        compiler_params=_cparams("arbitrary", "arbitrary"),
        name="s5_scan_rev" if reverse else "s5_scan_fwd",
    )(h_ctx3, h_lat3, a_re, a_im, wb, wc)


def _glu_kernel(x_ref, yf_ref, yr_ref, prm_ref, dv_ref, wa_ref, wb_ref, o_ref, *, rc):
    tm = x_ref.shape[0]
    prm = prm_ref[0]
    for c in range(tm // rc):
        rows = slice(c * rc, (c + 1) * rc)
        x = x_ref[rows, :]
        h = _pre_rows(x, prm)
        y = h * dv_ref[...] + yf_ref[rows, :] + yr_ref[rows, :]
        gy = jax.nn.gelu(y).astype(BF16)
        z = _dot(gy, wa_ref[...]) * jax.nn.sigmoid(_dot(gy, wb_ref[...]))
        o_ref[rows, :] = _post_rows(x, z, prm, 1.0)


def _glu(x2, yf2, yr2, prm, rows_per_sample, dv, wa, wb, o, *, tm, rc):
    n, d = x2.shape
    tm = min(tm, rows_per_sample)
    rc = min(rc, tm)
    tiles_per_sample = rows_per_sample // tm
    n_samples = prm.shape[0]
    return pl.pallas_call(
        functools.partial(_glu_kernel, rc=rc),
        grid=(n // tm,),
        in_specs=[
            pl.BlockSpec((tm, d), lambda i: (i, 0)),
            pl.BlockSpec((tm, d), lambda i: (i, 0)),
            pl.BlockSpec((tm, d), lambda i: (i, 0)),
            pl.BlockSpec((1, 8, d), lambda i: (jnp.minimum(i // tiles_per_sample, n_samples - 1), 0, 0)),
            pl.BlockSpec((None, 1, d), lambda i: (o, 0, 0)),
            pl.BlockSpec((None, d, d), lambda i: (o, 0, 0), pipeline_mode=pl.Buffered(1)),
            pl.BlockSpec((None, d, d), lambda i: (o, 0, 0), pipeline_mode=pl.Buffered(1)),
        ],
        out_specs=pl.BlockSpec((tm, d), lambda i: (i, 0)),
        out_shape=jax.ShapeDtypeStruct((n, d), F32),
        compiler_params=_cparams("arbitrary"),
        name="s5_glu",
    )(x2, yf2, yr2, prm, dv, wa, wb)


def _param_pack(m, g, i):
    s, _, d = m.shape
    rows = [m[:, 3 * i], m[:, 3 * i + 1], m[:, 3 * i + 2],
            jnp.broadcast_to(g[2 * i], (s, d)), jnp.broadcast_to(g[2 * i + 1], (s, d))]
    rows += [jnp.zeros((s, d), F32)] * 3
    return jnp.stack(rows, axis=1)


def _s5_discretise_kernel(lre_ref, lim_ref, ls_ref, bre_ref, bim_ref, are_ref, aim_ref, bbre_ref, bbim_ref):
    lam_re, lam_im = lre_ref[...], lim_ref[...]
    dt = jnp.exp(ls_ref[...])
    mag = jnp.exp(lam_re * dt)
    lbar_re = mag * jnp.cos(lam_im * dt)
    lbar_im = mag * jnp.sin(lam_im * dt)
    den = lam_re * lam_re + lam_im * lam_im
    nr = lbar_re - 1.0
    fr = (nr * lam_re + lbar_im * lam_im) / den
    fi = (lbar_im * lam_re - nr * lam_im) / den
    b_re, b_im = bre_ref[...], bim_ref[...]
    are_ref[...] = lbar_re
    aim_ref[...] = lbar_im
    bbre_ref[...] = fr * b_re - fi * b_im
    bbim_ref[...] = fr * b_im + fi * b_re


def _s5_discretise(lam_re, lam_im, log_step, b_re, b_im):
    nd, g, p, h = b_re.shape
    flat = (nd * g, p * h)
    rep = lambda a: jnp.broadcast_to(a[..., None], (nd, g, p, h)).reshape(flat)
    ls = jnp.broadcast_to(log_step[..., None, None], (nd, g, p, h)).reshape(flat)
    outs = pl.pallas_call(
        _s5_discretise_kernel,
        out_shape=[jax.ShapeDtypeStruct(flat, F32)] * 4,
        name="s5_discretise",
    )(rep(lam_re), rep(lam_im), ls, b_re.reshape(flat), b_im.reshape(flat))
    lbar_re, lbar_im, bbar_re, bbar_im = [o.reshape(nd, g, p, h) for o in outs]
    return lbar_re[..., 0], lbar_im[..., 0], bbar_re, bbar_im


def _s5_layout(lbar_re, lbar_im, bbar_re, bbar_im, c_re, c_im, nb, gsub, nsub):
    g, p, h = bbar_re.shape
    nblk = g // gsub
    eye = jnp.eye(gsub, dtype=F32)

    def drive(bb):
        bb = bb.reshape(nblk, gsub, p, h)
        return jnp.einsum('ngph,gk->nghkp', bb, eye).reshape(nblk, gsub * h, gsub * p)

    def read(cc):
        cc = cc.reshape(nblk, gsub, h, p)
        return jnp.einsum('nghp,gk->ngpkh', cc, eye).reshape(nblk, gsub * p, gsub * h)

    wb = jnp.concatenate([drive(bbar_re), drive(bbar_im)], axis=2)
    wc = jnp.concatenate([read(c_re), -read(c_im)], axis=1)
    nq = nblk // nsub
    wb = wb.reshape(nq, nsub, gsub * h, 2 * gsub * p).astype(BF16)
    wc = wc.reshape(nq, nsub, 2 * gsub * p, gsub * h).astype(BF16)
    a_re = jnp.broadcast_to(lbar_re.reshape(nq, 1, nsub * gsub * p), (nq, nb, nsub * gsub * p))
    a_im = jnp.broadcast_to(lbar_im.reshape(nq, 1, nsub * gsub * p), (nq, nb, nsub * gsub * p))
    return a_re, a_im, wb, wc


def kernel(x, c, ctx, c_ctx, w_mod, b_mod, norm_g, ffn_w_gate, ffn_w_up, ffn_w_down, mix_in, conv_w, pool_w, pool_scale, mix_out, s5_lambda_re, s5_lambda_im, s5_log_step, s5_b_re, s5_b_im, s5_c_re, s5_c_im, s5_d, glu_w_a, glu_w_b):
    b, l, d = x.shape
    lc = ctx.shape[1]
    depth = w_mod.shape[0]
    assert b == SUBLANES, "the S5 recurrence keeps the batch on the 8 sublanes"
    assert l % GRID_W == 0 and d % LANES == 0

    wg, wu, wd = ffn_w_gate.astype(BF16), ffn_w_up.astype(BF16), ffn_w_down.astype(BF16)
    w_in, w_out = mix_in.astype(BF16), mix_out.astype(BF16)
    pw = pool_w.astype(BF16)
    wa, wbb = glu_w_a.astype(BF16), glu_w_b.astype(BF16)
    conv_w8 = jnp.pad(conv_w, ((0, 0), (0, 8 - conv_w.shape[1]), (0, 0)))
    pscale = pool_scale[:, None, :]

    c_all = jnp.concatenate([c, c_ctx[None], jnp.zeros((2 * SUBLANES - b - 1, d), F32)], axis=0)
    m_all = _modulation(c_all, w_mod, b_mod)[:, :b + 1].reshape(depth, b + 1, N_MOD, d)

    x_lat = x.reshape(b * l, d)
    x_ctx = ctx.reshape(b * lc, d)
    ffn_lat = ffn_ctx = functools.partial(_ffn, tm=FFN_ROWS, tf=FFN_FF_COLS)

    for layer in range(depth):
        last = layer == depth - 1
        even = layer % 2 == 0
        run_ctx_in = not (last and even)
        ctx_out = not last
        g = norm_g[layer]
        m_lat, m_ctx = m_all[layer, :b], m_all[layer, b:]
        pk_lat = [_param_pack(m_lat, g, i) for i in range(3)]
        pk_ctx = [_param_pack(m_ctx, g, i) for i in range(3)]

        x_lat = ffn_lat(x_lat, pk_lat[0], l, wg, wu, wd, layer, 0)
        if run_ctx_in:
            x_ctx = ffn_ctx(x_ctx, pk_ctx[0], b * lc, wg, wu, wd, layer, 0)

        if even:
            e = layer // 2
            p_lat = _preproj(x_lat, pk_lat[1], l, w_in, e, tm=PROJ_ROWS, tn=PROJ_COLS)
            x_lat_new = _evenmix(p_lat, x_lat, pk_lat[1], l, conv_w8, pw, pscale, w_out, e, tm=MIX_ROWS, blk=GRID_W)
            if ctx_out:
                p_ctx = _preproj(x_ctx, pk_ctx[1], b * lc, w_in, e, tm=PROJ_ROWS, tn=PROJ_COLS)
                x_ctx = _evenmix(p_ctx, x_ctx, pk_ctx[1], lc, conv_w8, pw, pscale, w_out, e, tm=lc, blk=lc)
            x_lat = x_lat_new
        else:
            o = layer // 2
            h_lat = _prenorm(x_lat, pk_lat[1], l, tm=NORM_ROWS)
            h_ctx = _prenorm(x_ctx, pk_ctx[1], b * lc, tm=NORM_ROWS)
            lbr, lbi, bbr, bbi = _s5_discretise(s5_lambda_re[o], s5_lambda_im[o], s5_log_step[o], s5_b_re[o], s5_b_im[o])
            n_groups = lbr.shape[1]
            gsub = S5_GROUPS_PER_SUBBLOCK
            nsub = min(S5_SUBBLOCKS_PER_STEP, n_groups // gsub)
            ys = []
            for dr, rev in ((0, False), (1, True)):
                a_re, a_im, wb_s5, wc_s5 = _s5_layout(lbr[dr], lbi[dr], bbr[dr], bbi[dr], s5_c_re[o, dr], s5_c_im[o, dr],
                                                      b, gsub, nsub)
                ys.append(_s5_scan(h_ctx.reshape(b, lc, d), h_lat.reshape(b, l, d), a_re, a_im, wb_s5, wc_s5,
                                   reverse=rev, tc=S5_TIME_CHUNK))
            x_lat = _glu(x_lat, ys[0].reshape(b * l, d), ys[1].reshape(b * l, d), pk_lat[1], l, s5_d[:, None, :], wa, wbb, o,
                         tm=GLU_ROWS, rc=GLU_CHUNK)
            assert not ctx_out, "context output of an S5 layer is not needed by this block"

        x_lat = ffn_lat(x_lat, pk_lat[2], l, wg, wu, wd, layer, 1)
        if ctx_out:
            x_ctx = ffn_ctx(x_ctx, pk_ctx[2], b * lc, wg, wu, wd, layer, 1)

    return x_lat.reshape(b, l, d)
```

```python
import functools

import jax
import jax.numpy as jnp
from jax import lax
from jax.experimental import pallas as pl
from jax.experimental.pallas import tpu as pltpu

EPS = 1e-6
GRID_W = 64
POOL_WINDOWS = (2, 4, 8, 16)
N_MOD = 9

F32 = jnp.float32
BF16 = jnp.bfloat16

LANES = 128
SUBLANES = 8
VMEM_LIMIT_BYTES = 56 * 1024 * 1024
FFN_ROWS, FFN_FF_COLS = 1024, 512
FFN_EDGE_CHUNK = 256
PROJ_ROWS, PROJ_COLS = 1024, 1024
MIX_ROWS, MIX_CHUNK = 512, 512
NORM_ROWS = 1024
GLU_ROWS, GLU_CHUNK = 512, 256
S5_TIME_CHUNK = 32
S5_GROUPS_PER_SUBBLOCK = SUBLANES
S5_SUBBLOCKS_PER_STEP = 4
ROW_SHIFT, ROW_SCALE, ROW_GATE, ROW_GPRE, ROW_GPOST = 0, 1, 2, 3, 4


def _cparams(*sem):
    return pltpu.CompilerParams(dimension_semantics=sem, vmem_limit_bytes=VMEM_LIMIT_BYTES)


def _dot(a, b):
    return jnp.dot(a, b, preferred_element_type=F32)


def _pre_rows(x, prm):
    r = lax.rsqrt(jnp.mean(x * x, axis=-1, keepdims=True) + EPS)
    return (x * r * prm[ROW_GPRE:ROW_GPRE + 1]) * (1.0 + prm[ROW_SCALE:ROW_SCALE + 1]) + prm[ROW_SHIFT:ROW_SHIFT + 1]


def _post_rows(x, y, prm, weight):
    r = lax.rsqrt(jnp.mean(y * y, axis=-1, keepdims=True) + EPS)
    return x + (weight * prm[ROW_GATE:ROW_GATE + 1]) * (y * r * prm[ROW_GPOST:ROW_GPOST + 1])


def _rms_stats(src_ref, r_ref, rc, rows=None):
    tm, d = src_ref.shape

    def body(rows):
        v = src_ref[rows, :]
        ss = jnp.sum(v * v, axis=-1, keepdims=True)
        r_ref[rows, :] = jnp.broadcast_to(lax.rsqrt(ss / d + EPS), (rows.size, LANES))
    _row_chunks(tm, rc, body, unroll=2, rows=rows)


def _pre_apply(x_ref, r_ref, prm, h_ref, rc, rows=None):
    tm, d = x_ref.shape

    def body(rows):
        r = r_ref[rows, :]
        for k in range(d // LANES):
            cols = slice(k * LANES, (k + 1) * LANES)
            h = (x_ref[rows, cols] * r * prm[ROW_GPRE:ROW_GPRE + 1, cols]) * (1.0 + prm[ROW_SCALE:ROW_SCALE + 1, cols])
            h_ref[rows, cols] = (h + prm[ROW_SHIFT:ROW_SHIFT + 1, cols]).astype(h_ref.dtype)
    _row_chunks(tm, rc, body, rows=rows)


def _post_apply(x_ref, y_ref, r_ref, prm, o_ref, weight, rc, rows=None):
    tm, d = x_ref.shape

    def body(rows):
        r = r_ref[rows, :]
        for k in range(d // LANES):
            cols = slice(k * LANES, (k + 1) * LANES)
            yn = y_ref[rows, cols] * r * prm[ROW_GPOST:ROW_GPOST + 1, cols]
            o_ref[rows, cols] = x_ref[rows, cols] + (weight * prm[ROW_GATE:ROW_GATE + 1, cols]) * yn
    _row_chunks(tm, rc, body, rows=rows)


def _pick_tile(total, target, quantum=LANES):
    if total <= target:
        return total
    best = quantum
    for t in range(quantum, target + 1, quantum):
        if total % t == 0:
            best = t
    assert total % best == 0, (total, target, quantum)
    return best


def _row_chunks(total, chunk, body, unroll=1, rows=None):
    if rows is not None:
        start, size = rows
        chunk = min(chunk, size)
        for k in range(size // chunk):
            body(pl.ds(start + k * chunk, chunk))
        return
    chunk = min(chunk, total)

    def step(k, carry):
        body(pl.ds(pl.multiple_of(k * chunk, chunk), chunk))
        return carry

    lax.fori_loop(0, total // chunk, step, 0, unroll=unroll)


def _mod_kernel(c_ref, w_ref, b_ref, o_ref):
    sc = jax.nn.silu(c_ref[...]).astype(BF16)
    o_ref[0] = _dot(sc, w_ref[0].astype(BF16)) + b_ref[0]


def _modulation(c_all, w_mod, b_mod):
    depth, d, nd = w_mod.shape
    rows = c_all.shape[0]
    tn = _pick_tile(nd, 1024)
    return pl.pallas_call(
        _mod_kernel,
        grid=(depth, nd // tn),
        in_specs=[
            pl.BlockSpec((rows, d), lambda l, j: (0, 0)),
            pl.BlockSpec((1, d, tn), lambda l, j: (l, 0, j)),
            pl.BlockSpec((1, 1, tn), lambda l, j: (l, 0, j)),
        ],
        out_specs=pl.BlockSpec((1, rows, tn), lambda l, j: (l, 0, j)),
        out_shape=jax.ShapeDtypeStruct((depth, rows, nd), F32),
        compiler_params=_cparams("arbitrary", "arbitrary"),
        name="modulation",
    )(c_all, w_mod, b_mod.reshape(depth, 1, nd))


def _ffn_kernel(x_ref, prm_ref, wg_ref, wu_ref, wd_ref, o_ref, h_ref, r_ref, *, nj, rc, mc):
    j = pl.program_id(1)
    prm = prm_ref[0]
    tm = x_ref.shape[0]

    def swiglu_down(rows):
        h = h_ref[rows, :]
        t = (jax.nn.silu(_dot(h, wg_ref[...])) * _dot(h, wu_ref[...])).astype(BF16)
        return _dot(t, wd_ref[...])

    @pl.when(j == 0)
    def _():
        for c in range(tm // mc):
            _rms_stats(x_ref, r_ref, rc, rows=(c * mc, mc))
            _pre_apply(x_ref, r_ref, prm, h_ref, rc, rows=(c * mc, mc))
            o_ref[c * mc:(c + 1) * mc, :] = swiglu_down(slice(c * mc, (c + 1) * mc))

    @pl.when(jnp.logical_and(j > 0, j < nj - 1))
    def _():
        o_ref[...] += swiglu_down(slice(None))

    @pl.when(j == nj - 1)
    def _():
        for c in range(tm // mc):
            o_ref[c * mc:(c + 1) * mc, :] += swiglu_down(slice(c * mc, (c + 1) * mc))
            _rms_stats(o_ref, r_ref, rc, rows=(c * mc, mc))
            _post_apply(x_ref, o_ref, r_ref, prm, o_ref, 0.5, rc, rows=(c * mc, mc))


def _ffn(x2, prm, rows_per_sample, wg, wu, wd, l, k, *, tm, tf):
    n, d = x2.shape
    f = wg.shape[-1]
    tm = min(tm, rows_per_sample)
    tf = _pick_tile(f, tf)
    nj = f // tf
    tiles_per_sample = max(rows_per_sample // tm, 1)
    n_samples = prm.shape[0]
    assert nj >= 2, "the first and the last d_ff step are distinct code paths"
    kern = functools.partial(_ffn_kernel, nj=nj, rc=128, mc=min(FFN_EDGE_CHUNK, tm))
    return pl.pallas_call(
        kern,
        grid=(n // tm, nj),
        in_specs=[
            pl.BlockSpec((tm, d), lambda i, j: (i, 0)),
            pl.BlockSpec((1, 8, d), lambda i, j: (jnp.minimum(i // tiles_per_sample, n_samples - 1), 0, 0)),
            pl.BlockSpec((None, None, d, tf), lambda i, j: (l, k, 0, j)),
            pl.BlockSpec((None, None, d, tf), lambda i, j: (l, k, 0, j)),
            pl.BlockSpec((None, None, tf, d), lambda i, j: (l, k, j, 0)),
        ],
        out_specs=pl.BlockSpec((tm, d), lambda i, j: (i, 0)),
        out_shape=jax.ShapeDtypeStruct((n, d), F32),
        scratch_shapes=[pltpu.VMEM((tm, d), BF16), pltpu.VMEM((tm, LANES), F32)],
        compiler_params=_cparams("arbitrary", "arbitrary"),
        name="ffn",
    )(x2, prm, wg, wu, wd)


def _preproj_kernel(x_ref, prm_ref, w_ref, o_ref, h_ref, r_ref, *, rc, mc):
    j = pl.program_id(1)
    tm = x_ref.shape[0]

    @pl.when(j == 0)
    def _():
        for c in range(tm // mc):
            _rms_stats(x_ref, r_ref, rc, rows=(c * mc, mc))
            _pre_apply(x_ref, r_ref, prm_ref[0], h_ref, rc, rows=(c * mc, mc))
            o_ref[c * mc:(c + 1) * mc, :] = _dot(h_ref[c * mc:(c + 1) * mc, :], w_ref[...])

    @pl.when(j > 0)
    def _():
        o_ref[...] = _dot(h_ref[...], w_ref[...])


def _preproj(x2, prm, rows_per_sample, w, e, *, tm, tn):
    n, d = x2.shape
    nc = w.shape[-1]
    tm = min(tm, rows_per_sample)
    tn = _pick_tile(nc, tn)
    tiles_per_sample = max(rows_per_sample // tm, 1)
    n_samples = prm.shape[0]
    return pl.pallas_call(
        functools.partial(_preproj_kernel, rc=128, mc=min(FFN_EDGE_CHUNK, tm)),
        grid=(n // tm, nc // tn),
        in_specs=[
            pl.BlockSpec((tm, d), lambda i, j: (i, 0)),
            pl.BlockSpec((1, 8, d), lambda i, j: (jnp.minimum(i // tiles_per_sample, n_samples - 1), 0, 0)),
            pl.BlockSpec((None, d, tn), lambda i, j: (e, 0, j)),
        ],
        out_specs=pl.BlockSpec((tm, tn), lambda i, j: (i, j)),
        out_shape=jax.ShapeDtypeStruct((n, nc), F32),
        scratch_shapes=[pltpu.VMEM((tm, d), BF16), pltpu.VMEM((tm, LANES), F32)],
        compiler_params=_cparams("arbitrary", "arbitrary"),
        name="mix_in_proj",
    )(x2, prm, w)


def _evenmix_kernel(p_ref, pprev_ref, pnext_ref, x_ref, prm_ref, cw_ref, pw_ref, ps_ref, wo_ref, o_ref,
                    cv_ref, pa_ref, pb_ref, ycat_ref, *, blk, mc, tiles_per_sample, mix_a, pool_ch):
    i = pl.program_id(0)
    tm = x_ref.shape[0]
    prm = prm_ref[0]
    nblk = mc // blk
    pitch = blk + 2 * SUBLANES
    npad = nblk * pitch + SUBLANES
    inner = npad - 2 * SUBLANES
    t_in_sample = i % tiles_per_sample
    has_prev = (t_in_sample != 0).astype(F32)
    has_next = (t_in_sample != tiles_per_sample - 1).astype(F32)
    zeros8 = jnp.zeros((SUBLANES, pool_ch), F32)

    c0, c1, c2, c3 = 0, mix_a, 2 * mix_a, 3 * mix_a
    cv_ref[SUBLANES:SUBLANES + tm, :] = p_ref[:, c1:c2] * p_ref[:, c2:c3]
    cv_ref[0:SUBLANES, :] = pprev_ref[:, c1:c2] * pprev_ref[:, c2:c3] * has_prev
    cv_ref[SUBLANES + tm:2 * SUBLANES + tm, :] = pnext_ref[:, c1:c2] * pnext_ref[:, c2:c3] * has_next
    pos = lax.broadcasted_iota(jnp.int32, (mc, 1), 0) % blk
    pb_ref[0:SUBLANES, :] = zeros8
    pb_ref[npad - SUBLANES:npad, :] = zeros8

    for c in range(tm // mc):
        r0 = c * mc
        rows = slice(r0, r0 + mc)
        conv = (cw_ref[0:1, :] * cv_ref[SUBLANES - 1 + r0:SUBLANES - 1 + r0 + mc, :]
                + cw_ref[1:2, :] * cv_ref[SUBLANES + r0:SUBLANES + r0 + mc, :]
                + cw_ref[2:3, :] * cv_ref[SUBLANES + 1 + r0:SUBLANES + 1 + r0 + mc, :])
        ycat_ref[rows, 0:mix_a] = (p_ref[rows, c0:c1] * conv).astype(BF16)

        for gi, w in enumerate(POOL_WINDOWS):
            lo, hi = c3 + gi * pool_ch, c3 + (gi + 1) * pool_ch
            pa_ref[0:SUBLANES, :] = zeros8
            for b in range(nblk):
                base = b * pitch + SUBLANES
                pa_ref[base:base + blk, :] = p_ref[r0 + b * blk:r0 + (b + 1) * blk, lo:hi]
                pa_ref[base + blk:base + blk + SUBLANES, :] = zeros8
                pa_ref[base + blk + SUBLANES:base + blk + 2 * SUBLANES, :] = zeros8
            offsets = [(SUBLANES - 1, SUBLANES)]
            half = 1
            while 4 * half <= w:
                offsets.append((SUBLANES - half, SUBLANES + half))
                half *= 2
            src, dst = pa_ref, pb_ref
            for lo_off, hi_off in offsets:
                dst[SUBLANES:SUBLANES + inner, :] = src[lo_off:lo_off + inner, :] + src[hi_off:hi_off + inner, :]
                src, dst = dst, src
            cnt = (jnp.minimum(pos + w // 2, blk) - jnp.maximum(pos - w // 2, 0)).astype(F32)
            pieces = [src[b * pitch + SUBLANES:b * pitch + SUBLANES + blk, :] for b in range(nblk)]
            total = pieces[0] if nblk == 1 else jnp.concatenate(pieces, axis=0)
            pooled = total / cnt - p_ref[rows, lo:hi]
            yb = _dot(pooled.astype(BF16), pw_ref[gi]) * ps_ref[:, gi * pool_ch:(gi + 1) * pool_ch]
            ycat_ref[rows, mix_a + gi * pool_ch:mix_a + (gi + 1) * pool_ch] = yb.astype(BF16)

        y = _dot(ycat_ref[rows, :], wo_ref[...])
        o_ref[rows, :] = _post_rows(x_ref[rows, :], y, prm, 1.0)


def _evenmix(p2, x2, prm, rows_per_sample, conv_w8, pool_w, pool_scale, w_out, e, *, tm, blk):
    n, d = x2.shape
    nc = p2.shape[1]
    mix_a = conv_w8.shape[-1]
    pool_ch = pool_w.shape[-1]
    mix_b = pool_ch * len(POOL_WINDOWS)
    tm = min(tm, rows_per_sample)
    tiles_per_sample = rows_per_sample // tm
    n_samples = prm.shape[0]
    mc = max(blk, min(MIX_CHUNK, tm))
    assert tm % mc == 0 and mc % blk == 0
    npad = (mc // blk) * (blk + 2 * SUBLANES) + SUBLANES
    tb = tm // SUBLANES
    nb8 = n // SUBLANES
    kern = functools.partial(_evenmix_kernel, blk=blk, mc=mc, tiles_per_sample=tiles_per_sample, mix_a=mix_a,
                             pool_ch=pool_ch)
    return pl.pallas_call(
        kern,
        grid=(n // tm,),
        in_specs=[
            pl.BlockSpec((tm, nc), lambda i: (i, 0)),
            pl.BlockSpec((SUBLANES, nc), lambda i: (jnp.maximum(i * tb - 1, 0), 0)),
            pl.BlockSpec((SUBLANES, nc), lambda i: (jnp.minimum((i + 1) * tb, nb8 - 1), 0)),
            pl.BlockSpec((tm, d), lambda i: (i, 0)),
            pl.BlockSpec((1, 8, d), lambda i: (jnp.minimum(i // tiles_per_sample, n_samples - 1), 0, 0)),
            pl.BlockSpec((None, 8, mix_a), lambda i: (e, 0, 0)),
            pl.BlockSpec((None, len(POOL_WINDOWS), pool_ch, pool_ch), lambda i: (e, 0, 0, 0)),
            pl.BlockSpec((None, 1, mix_b), lambda i: (e, 0, 0)),
            pl.BlockSpec((None, mix_a + mix_b, d), lambda i: (e, 0, 0), pipeline_mode=pl.Buffered(1)),
        ],
        out_specs=pl.BlockSpec((tm, d), lambda i: (i, 0)),
        out_shape=jax.ShapeDtypeStruct((n, d), F32),
        scratch_shapes=[
            pltpu.VMEM((tm + 2 * SUBLANES, mix_a), F32),
            pltpu.VMEM((npad, pool_ch), F32),
            pltpu.VMEM((npad, pool_ch), F32),
            pltpu.VMEM((tm, mix_a + mix_b), BF16),
        ],
        compiler_params=_cparams("arbitrary"),
        name="even_mixer",
    )(p2, p2, p2, x2, prm, conv_w8, pool_w, pool_scale, w_out)


def _prenorm_kernel(x_ref, prm_ref, o_ref, r_ref, *, rc):
    _rms_stats(x_ref, r_ref, rc)
    _pre_apply(x_ref, r_ref, prm_ref[0], o_ref, rc)


def _prenorm(x2, prm, rows_per_sample, *, tm):
    n, d = x2.shape
    tm = min(tm, rows_per_sample)
    tiles_per_sample = rows_per_sample // tm
    n_samples = prm.shape[0]
    return pl.pallas_call(
        functools.partial(_prenorm_kernel, rc=128),
        grid=(n // tm,),
        in_specs=[
            pl.BlockSpec((tm, d), lambda i: (i, 0)),
            pl.BlockSpec((1, 8, d), lambda i: (jnp.minimum(i // tiles_per_sample, n_samples - 1), 0, 0)),
        ],
        out_specs=pl.BlockSpec((tm, d), lambda i: (i, 0)),
        out_shape=jax.ShapeDtypeStruct((n, d), BF16),
        scratch_shapes=[pltpu.VMEM((tm, LANES), F32)],
        compiler_params=_cparams("arbitrary"),
        name="s5_prenorm",
    )(x2, prm)


def _s5_drive_pieces(j, hc_ref, hl_ref, use_ctx, wb_ref, slab_ref, hbf_ref, buf_ref, *, tc, ch_sub, st_sub, npiece):
    nb = hl_ref.shape[0]
    pitch = tc + SUBLANES
    ch = slice(j * ch_sub, (j + 1) * ch_sub)

    def relayout():
        for k in range(j * ch_sub // LANES, (j + 1) * ch_sub // LANES):
            cols = slice(k * LANES, (k + 1) * LANES)
            for b in range(nb):
                hb = jnp.where(use_ctx, hc_ref[b, :, cols], hl_ref[b, :, cols])
                slab_ref[k, b * pitch:b * pitch + tc, :] = hb.astype(F32)
            for t in range(0, tc, 2):
                pair = [slab_ref.at[k][pl.ds(t + i, nb, stride=pitch), :] for i in range(2)]
                hbf_ref[t * nb:(t + 2) * nb, cols] = jnp.concatenate(pair, axis=0).astype(BF16)

    def matmul(p):
        w = st_sub // npiece
        cols = slice(j * st_sub + p * w, j * st_sub + (p + 1) * w)
        buf_ref[:, cols] = _dot(hbf_ref[:, ch], wb_ref[0, j, :, p * w:(p + 1) * w])

    return [relayout] + [functools.partial(matmul, p) for p in range(npiece)]


def _s5_readout_pieces(j, wc_ref, buf_ref, slab_ref, o_ref, *, tc, ch_sub, st_sub, npiece):
    nb = o_ref.shape[0]
    ch = slice(j * ch_sub, (j + 1) * ch_sub)

    def matmul(p):
        m = tc * nb // npiece
        rows = slice(p * m, (p + 1) * m)
        y = _dot(buf_ref[rows, j * st_sub:(j + 1) * st_sub], wc_ref[0, j])
        for i, k in enumerate(range(j * ch_sub // LANES, (j + 1) * ch_sub // LANES)):
            slab_ref[k, rows, :] = y[:, i * LANES:(i + 1) * LANES]

    def relayout():
        for k in range(j * ch_sub // LANES, (j + 1) * ch_sub // LANES):
            cols = slice(k * LANES, (k + 1) * LANES)
            for b in range(nb):
                o_ref[b, :, cols] = slab_ref.at[k][pl.ds(b, tc, stride=nb), :]

    return [functools.partial(matmul, p) for p in range(npiece)] + [relayout]


def _s5_scan_steps(j, sr, si, ar, ai, steps, buf_ref, sbf_ref, *, reverse, tc, st_sub, nb):
    half = st_sub // 2
    lo = j * st_sub
    assert len(steps) % 2 == 0 and steps[0] % 2 == 0
    for s0 in steps[::2]:
        pair = []
        for s in (s0, s0 + 1):
            t = (tc - 1 - s) if reverse else s
            rows = slice(t * nb, (t + 1) * nb)
            nr = ar * sr - ai * si + buf_ref[rows, lo:lo + half]
            ni = ar * si + ai * sr + buf_ref[rows, lo + half:lo + st_sub]
            sr, si = nr, ni
            pair.append((t, nr, ni))
        pair.sort(key=lambda e: e[0])
        rows2 = slice(pair[0][0] * nb, (pair[0][0] + 2) * nb)
        sbf_ref[rows2, lo:lo + half] = jnp.concatenate([pair[0][1], pair[1][1]], axis=0).astype(BF16)
        sbf_ref[rows2, lo + half:lo + st_sub] = jnp.concatenate([pair[0][2], pair[1][2]], axis=0).astype(BF16)
    return sr, si


def _s5_direction_step(r, use_ctx, hc_ref, hl_ref, are_ref, aim_ref, wb_ref, wc_ref, o_ref,
                       slab_in_ref, slab_out_ref, hbf_ref, bufs, sbfs, st_ref, *, reverse, tc, nsub, ch_sub, st_sub):
    nb = st_ref.shape[0]
    half = st_sub // 2
    drive_buf, scan_buf, state_buf, read_buf = bufs[r], bufs[1 - r], sbfs[1 - r], sbfs[r]
    scan_kw = dict(reverse=reverse, tc=tc, st_sub=st_sub, nb=nb)
    for j in range(nsub):
        re_cols = slice(2 * j * half, (2 * j + 1) * half)
        im_cols = slice((2 * j + 1) * half, (2 * j + 2) * half)
        ar = are_ref[0, :, j * half:(j + 1) * half]
        ai = aim_ref[0, :, j * half:(j + 1) * half]
        sr, si = st_ref[:, re_cols], st_ref[:, im_cols]
        dp = _s5_drive_pieces(j, hc_ref, hl_ref, use_ctx, wb_ref, slab_in_ref, hbf_ref, drive_buf,
                              tc=tc, ch_sub=ch_sub, st_sub=st_sub, npiece=4)
        rp = _s5_readout_pieces(j, wc_ref, read_buf, slab_out_ref, o_ref,
                                tc=tc, ch_sub=ch_sub, st_sub=st_sub, npiece=2)
        pieces = [dp[0], rp[0], dp[1], dp[2], rp[1], dp[3], dp[4], rp[2]]
        per = tc // len(pieces)
        for g, piece in enumerate(pieces):
            last = tc if g == len(pieces) - 1 else (g + 1) * per
            sr, si = _s5_scan_steps(j, sr, si, ar, ai, range(g * per, last), scan_buf, state_buf, **scan_kw)
            piece()
        st_ref[:, re_cols] = sr
        st_ref[:, im_cols] = si


def _s5_kernel(*refs, n_ctx_chunks, tc, nsub, ch_sub, st_sub):
    ins, outs, scr = refs[:12], refs[12:14], refs[14:]
    s = pl.program_id(1)
    dirs = []
    for d in range(2):
        hc_ref, hl_ref, are_ref, aim_ref, wb_ref, wc_ref = ins[6 * d:6 * d + 6]
        slab_in, slab_out, hbf, buf0, buf1, sbf0, sbf1, st = scr[8 * d:8 * d + 8]
        dirs.append((hc_ref, hl_ref, are_ref, aim_ref, wb_ref, wc_ref, outs[d], slab_in, slab_out, hbf,
                     (buf0, buf1), (sbf0, sbf1), st))

    @pl.when(s == 0)
    def _():
        for dd in dirs:
            for r in dd[10] + dd[11]:
                r[...] = jnp.zeros_like(r)
            dd[12][...] = jnp.zeros_like(dd[12])

    use_ctx = s < n_ctx_chunks
    for r in range(2):
        @pl.when(s % 2 == r)
        def _(r=r):
            for d, dd in enumerate(dirs):
                _s5_direction_step(r, use_ctx, *dd, reverse=bool(d), tc=tc, nsub=nsub, ch_sub=ch_sub, st_sub=st_sub)


def _s5_scan(h_ctx3, h_lat3, params_fwd, params_rev, *, tc):
    nb, lc, d = h_ctx3.shape
    l = h_lat3.shape[1]
    nq, nsub, ch_sub, st_sub = params_fwd[2].shape
    dq = nsub * ch_sub
    tc = min(tc, lc)
    ncc, nlc = lc // tc, l // tc
    pitch = tc + SUBLANES
    n_chunks = ncc + nlc

    def ctx_idx(k, reverse):
        cc = jnp.clip(k, 0, ncc - 1)
        return (ncc - 1 - cc) if reverse else cc

    def lat_idx(k, reverse):
        lc_ = jnp.clip(k - ncc, 0, nlc - 1)
        return (nlc - 1 - lc_) if reverse else lc_

    def dir_in_specs(reverse):
        return [
            pl.BlockSpec((nb, tc, dq), lambda q, s: (0, ctx_idx(jnp.minimum(s, n_chunks - 1), reverse), q)),
            pl.BlockSpec((nb, tc, dq), lambda q, s: (0, lat_idx(jnp.minimum(s, n_chunks - 1), reverse), q)),
            pl.BlockSpec((1, nb, nsub * st_sub // 2), lambda q, s: (q, 0, 0)),
            pl.BlockSpec((1, nb, nsub * st_sub // 2), lambda q, s: (q, 0, 0)),
            pl.BlockSpec((1, nsub, ch_sub, st_sub), lambda q, s: (q, 0, 0, 0)),
            pl.BlockSpec((1, nsub, st_sub, ch_sub), lambda q, s: (q, 0, 0, 0)),
        ]

    def dir_out_spec(reverse):
        return pl.BlockSpec((nb, tc, dq), lambda q, s: (0, lat_idx(s - 2, reverse), q))

    drive_buf = pltpu.VMEM((tc * nb, nsub * st_sub), F32)
    state_buf = pltpu.VMEM((tc * nb, nsub * st_sub), BF16)
    dir_scratch = [
        pltpu.VMEM((dq // LANES, nb * pitch, LANES), F32),
        pltpu.VMEM((dq // LANES, nb * tc, LANES), F32),
        pltpu.VMEM((tc * nb, dq), BF16),
        drive_buf, drive_buf,
        state_buf, state_buf,
        pltpu.VMEM((nb, nsub * st_sub), F32),
    ]
    kern = functools.partial(_s5_kernel, n_ctx_chunks=ncc, tc=tc, nsub=nsub, ch_sub=ch_sub, st_sub=st_sub)
    y_shape = jax.ShapeDtypeStruct((nb, l, d), F32)
    return pl.pallas_call(
        kern,
        grid=(nq, n_chunks + 2),
        in_specs=dir_in_specs(False) + dir_in_specs(True),
        out_specs=[dir_out_spec(False), dir_out_spec(True)],
        out_shape=[y_shape, y_shape],
        scratch_shapes=dir_scratch + dir_scratch,
        compiler_params=_cparams("arbitrary", "arbitrary"),
        name="s5_scan",
    )(h_ctx3, h_lat3, *params_fwd, h_ctx3, h_lat3, *params_rev)


def _glu_kernel(x_ref, yf_ref, yr_ref, prm_ref, dv_ref, wa_ref, wb_ref, o_ref, *, rc):
    tm = x_ref.shape[0]
    prm = prm_ref[0]
    for c in range(tm // rc):
        rows = slice(c * rc, (c + 1) * rc)
        x = x_ref[rows, :]
        h = _pre_rows(x, prm)
        y = h * dv_ref[...] + yf_ref[rows, :] + yr_ref[rows, :]
        gy = jax.nn.gelu(y).astype(BF16)
        z = _dot(gy, wa_ref[...]) * jax.nn.sigmoid(_dot(gy, wb_ref[...]))
        o_ref[rows, :] = _post_rows(x, z, prm, 1.0)


def _glu(x2, yf2, yr2, prm, rows_per_sample, dv, wa, wb, o, *, tm, rc):
    n, d = x2.shape
    tm = min(tm, rows_per_sample)
    rc = min(rc, tm)
    tiles_per_sample = rows_per_sample // tm
    n_samples = prm.shape[0]
    return pl.pallas_call(
        functools.partial(_glu_kernel, rc=rc),
        grid=(n // tm,),
        in_specs=[
            pl.BlockSpec((tm, d), lambda i: (i, 0)),
            pl.BlockSpec((tm, d), lambda i: (i, 0)),
            pl.BlockSpec((tm, d), lambda i: (i, 0)),
            pl.BlockSpec((1, 8, d), lambda i: (jnp.minimum(i // tiles_per_sample, n_samples - 1), 0, 0)),
            pl.BlockSpec((None, 1, d), lambda i: (o, 0, 0)),
            pl.BlockSpec((None, d, d), lambda i: (o, 0, 0), pipeline_mode=pl.Buffered(1)),
            pl.BlockSpec((None, d, d), lambda i: (o, 0, 0), pipeline_mode=pl.Buffered(1)),
        ],
        out_specs=pl.BlockSpec((tm, d), lambda i: (i, 0)),
        out_shape=jax.ShapeDtypeStruct((n, d), F32),
        compiler_params=_cparams("arbitrary"),
        name="s5_glu",
    )(x2, yf2, yr2, prm, dv, wa, wb)


def _param_pack(m, g, i):
    s, _, d = m.shape
    rows = [m[:, 3 * i], m[:, 3 * i + 1], m[:, 3 * i + 2],
            jnp.broadcast_to(g[2 * i], (s, d)), jnp.broadcast_to(g[2 * i + 1], (s, d))]
    rows += [jnp.zeros((s, d), F32)] * 3
    return jnp.stack(rows, axis=1)


def _s5_discretise_kernel(lre_ref, lim_ref, ls_ref, bre_ref, bim_ref, are_ref, aim_ref, bbre_ref, bbim_ref):
    lam_re, lam_im = lre_ref[...], lim_ref[...]
    dt = jnp.exp(ls_ref[...])
    mag = jnp.exp(lam_re * dt)
    lbar_re = mag * jnp.cos(lam_im * dt)
    lbar_im = mag * jnp.sin(lam_im * dt)
    den = lam_re * lam_re + lam_im * lam_im
    nr = lbar_re - 1.0
    fr = (nr * lam_re + lbar_im * lam_im) / den
    fi = (lbar_im * lam_re - nr * lam_im) / den
    b_re, b_im = bre_ref[...], bim_ref[...]
    are_ref[...] = lbar_re
    aim_ref[...] = lbar_im
    bbre_ref[...] = fr * b_re - fi * b_im
    bbim_ref[...] = fr * b_im + fi * b_re


def _s5_discretise(lam_re, lam_im, log_step, b_re, b_im):
    nd, g, p, h = b_re.shape
    flat = (nd * g, p * h)
    rep = lambda a: jnp.broadcast_to(a[..., None], (nd, g, p, h)).reshape(flat)
    ls = jnp.broadcast_to(log_step[..., None, None], (nd, g, p, h)).reshape(flat)
    outs = pl.pallas_call(
        _s5_discretise_kernel,
        out_shape=[jax.ShapeDtypeStruct(flat, F32)] * 4,
        name="s5_discretise",
    )(rep(lam_re), rep(lam_im), ls, b_re.reshape(flat), b_im.reshape(flat))
    lbar_re, lbar_im, bbar_re, bbar_im = [o.reshape(nd, g, p, h) for o in outs]
    return lbar_re[..., 0], lbar_im[..., 0], bbar_re, bbar_im


def _s5_layout(lbar_re, lbar_im, bbar_re, bbar_im, c_re, c_im, nb, gsub, nsub):
    g, p, h = bbar_re.shape
    nblk = g // gsub
    eye = jnp.eye(gsub, dtype=F32)

    def drive(bb):
        bb = bb.reshape(nblk, gsub, p, h)
        return jnp.einsum('ngph,gk->nghkp', bb, eye).reshape(nblk, gsub * h, gsub * p)

    def read(cc):
        cc = cc.reshape(nblk, gsub, h, p)
        return jnp.einsum('nghp,gk->ngpkh', cc, eye).reshape(nblk, gsub * p, gsub * h)

    wb = jnp.concatenate([drive(bbar_re), drive(bbar_im)], axis=2)
    wc = jnp.concatenate([read(c_re), -read(c_im)], axis=1)
    nq = nblk // nsub
    wb = wb.reshape(nq, nsub, gsub * h, 2 * gsub * p).astype(BF16)
    wc = wc.reshape(nq, nsub, 2 * gsub * p, gsub * h).astype(BF16)
    a_re = jnp.broadcast_to(lbar_re.reshape(nq, 1, nsub * gsub * p), (nq, nb, nsub * gsub * p))
    a_im = jnp.broadcast_to(lbar_im.reshape(nq, 1, nsub * gsub * p), (nq, nb, nsub * gsub * p))
    return a_re, a_im, wb, wc


def kernel(x, c, ctx, c_ctx, w_mod, b_mod, norm_g, ffn_w_gate, ffn_w_up, ffn_w_down, mix_in, conv_w, pool_w, pool_scale, mix_out, s5_lambda_re, s5_lambda_im, s5_log_step, s5_b_re, s5_b_im, s5_c_re, s5_c_im, s5_d, glu_w_a, glu_w_b):
    b, l, d = x.shape
    lc = ctx.shape[1]
    depth = w_mod.shape[0]
    assert b == SUBLANES, "the S5 recurrence keeps the batch on the 8 sublanes"
    assert l % GRID_W == 0 and d % LANES == 0

    wg, wu, wd = ffn_w_gate.astype(BF16), ffn_w_up.astype(BF16), ffn_w_down.astype(BF16)
    w_in, w_out = mix_in.astype(BF16), mix_out.astype(BF16)
    pw = pool_w.astype(BF16)
    wa, wbb = glu_w_a.astype(BF16), glu_w_b.astype(BF16)
    conv_w8 = jnp.pad(conv_w, ((0, 0), (0, 8 - conv_w.shape[1]), (0, 0)))
    pscale = pool_scale[:, None, :]

    c_all = jnp.concatenate([c, c_ctx[None], jnp.zeros((2 * SUBLANES - b - 1, d), F32)], axis=0)
    m_all = _modulation(c_all, w_mod, b_mod)[:, :b + 1].reshape(depth, b + 1, N_MOD, d)

    x_lat = x.reshape(b * l, d)
    x_ctx = ctx.reshape(b * lc, d)
    ffn_lat = ffn_ctx = functools.partial(_ffn, tm=FFN_ROWS, tf=FFN_FF_COLS)

    for layer in range(depth):
        last = layer == depth - 1
        even = layer % 2 == 0
        run_ctx_in = not (last and even)
        ctx_out = not last
        g = norm_g[layer]
        m_lat, m_ctx = m_all[layer, :b], m_all[layer, b:]
        pk_lat = [_param_pack(m_lat, g, i) for i in range(3)]
        pk_ctx = [_param_pack(m_ctx, g, i) for i in range(3)]

        x_lat = ffn_lat(x_lat, pk_lat[0], l, wg, wu, wd, layer, 0)
        if run_ctx_in:
            x_ctx = ffn_ctx(x_ctx, pk_ctx[0], b * lc, wg, wu, wd, layer, 0)

        if even:
            e = layer // 2
            p_lat = _preproj(x_lat, pk_lat[1], l, w_in, e, tm=PROJ_ROWS, tn=PROJ_COLS)
            x_lat_new = _evenmix(p_lat, x_lat, pk_lat[1], l, conv_w8, pw, pscale, w_out, e, tm=MIX_ROWS, blk=GRID_W)
            if ctx_out:
                p_ctx = _preproj(x_ctx, pk_ctx[1], b * lc, w_in, e, tm=PROJ_ROWS, tn=PROJ_COLS)
                x_ctx = _evenmix(p_ctx, x_ctx, pk_ctx[1], lc, conv_w8, pw, pscale, w_out, e, tm=lc, blk=lc)
            x_lat = x_lat_new
        else:
            o = layer // 2
            h_lat = _prenorm(x_lat, pk_lat[1], l, tm=NORM_ROWS)
            h_ctx = _prenorm(x_ctx, pk_ctx[1], b * lc, tm=NORM_ROWS)
            lbr, lbi, bbr, bbi = _s5_discretise(s5_lambda_re[o], s5_lambda_im[o], s5_log_step[o], s5_b_re[o], s5_b_im[o])
            n_groups = lbr.shape[1]
            gsub = S5_GROUPS_PER_SUBBLOCK
            nsub = min(S5_SUBBLOCKS_PER_STEP, n_groups // gsub)
            prm_s5 = [_s5_layout(lbr[dr], lbi[dr], bbr[dr], bbi[dr], s5_c_re[o, dr], s5_c_im[o, dr], b, gsub, nsub)
                      for dr in range(2)]
            ys = _s5_scan(h_ctx.reshape(b, lc, d), h_lat.reshape(b, l, d), prm_s5[0], prm_s5[1], tc=S5_TIME_CHUNK)
            x_lat = _glu(x_lat, ys[0].reshape(b * l, d), ys[1].reshape(b * l, d), pk_lat[1], l, s5_d[:, None, :], wa, wbb, o,
                         tm=GLU_ROWS, rc=GLU_CHUNK)
            assert not ctx_out, "context output of an S5 layer is not needed by this block"

        x_lat = ffn_lat(x_lat, pk_lat[2], l, wg, wu, wd, layer, 1)
        if ctx_out:
            x_ctx = ffn_ctx(x_ctx, pk_ctx[2], b * lc, wg, wu, wd, layer, 1)

    return x_lat.reshape(b, l, d)
```
